```python
import math
import jax, jax.numpy as jnp
from jax import lax
import numpy as np

D_MODEL = 4096
BATCH = 2
SEQ = 4096
DEPTH = 2

CHUNK = 64
N_META = 16
Q_BLOCK = 128
HEAD_DIM = 128
A_HEADS = D_MODEL // (2 * HEAD_DIM)
B_HEADS = D_MODEL // (2 * HEAD_DIM)
B_KV_HEADS = 2
B_KV_LATENT = 512
IDX_HEADS = 16
IDX_DIM = 64
TOPK_MAX = 256
REL_BUCKETS = 32
REL_MAX_DIST = 128
D_FF = 256 * ((8 * D_MODEL // 3 + 255) // 256)
N_EXPERTS = 8
TOP_K_EXPERTS = 2
D_FF_EXPERT = D_MODEL
N_DENSE = (DEPTH + 1) // 2
N_MOE = DEPTH // 2
EPS = 1e-6
NEG = -1e30

A_QW = A_HEADS * HEAD_DIM
B_QW = B_HEADS * HEAD_DIM
IDX_QW = IDX_HEADS * IDX_DIM
IN_SIZES = (A_QW, A_QW, A_QW, A_HEADS,
            B_QW, B_KV_LATENT, IDX_QW, IDX_DIM, IDX_HEADS)
IN_COLS = sum(IN_SIZES)
SPLIT_POINTS = tuple(int(v) for v in np.cumsum(IN_SIZES)[:-1])

kernel_name = "hybrid_fox_dsa_gated_moe_encoder"


def rmsnorm(x, g):
    xf = x.astype(jnp.float32)
    y = xf * lax.rsqrt(jnp.mean(xf * xf, axis=-1, keepdims=True) + EPS)
    return (y * g.astype(jnp.float32)).astype(x.dtype)


def chunk_ids(t_pad):
    p = jnp.arange(t_pad)
    return jnp.where(p < N_META, 0, 1 + (p - N_META) // CHUNK)


def rel_bucket(rel):
    nb = REL_BUCKETS // 2
    max_exact = nb // 2
    ret = jnp.where(rel > 0, nb, 0)
    n = jnp.abs(rel)
    nf = jnp.maximum(n, 1).astype(jnp.float32)
    large = max_exact + (jnp.log(nf / max_exact) / math.log(REL_MAX_DIST / max_exact)
                         * (nb - max_exact)).astype(jnp.int32)
    large = jnp.minimum(large, nb - 1)
    return ret + jnp.where(n < max_exact, n, large)


def to_blocks(a):
    b, t = a.shape[:2]
    return jnp.moveaxis(a.reshape(b, t // Q_BLOCK, Q_BLOCK, *a.shape[2:]), 1, 0)


def from_blocks(a):
    nb, b, qb = a.shape[:3]
    return jnp.moveaxis(a, 0, 1).reshape(b, nb * qb, *a.shape[3:])


def forgetting_attention(q, k, v, log_f):
    t_len, dh = q.shape[1], q.shape[3]
    c = jnp.cumsum(log_f, axis=1)
    c_keys = jnp.transpose(c, (0, 2, 1))
    pos = jnp.arange(t_len)
    scale = dh ** -0.5

    def block(args):
        qb, cb, i = args
        qpos = i * Q_BLOCK + jnp.arange(Q_BLOCK)
        s = jnp.einsum('bqhd,bkhd->bhqk', qb, k).astype(jnp.float32) * scale
        s = s + jnp.transpose(cb, (0, 2, 1))[..., None] - c_keys[:, :, None, :]
        s = jnp.where(pos[None, :] <= qpos[:, None], s, NEG)
        p = jax.nn.softmax(s, axis=-1)
        return jnp.einsum('bhqk,bkhd->bqhd', p.astype(v.dtype), v)

    nb = t_len // Q_BLOCK
    out = lax.map(block, (to_blocks(q), to_blocks(c), jnp.arange(nb)))
    return from_blocks(out)


def dsa_attention(q, k, v, q_idx, k_idx, w_idx, rel_bias, cid, topk):
    b_sz, t_len, n_heads, dh = q.shape
    g = k.shape[2]
    r = n_heads // g
    scale = dh ** -0.5
    idx_scale = IDX_DIM ** -0.5
    bidx = jnp.arange(b_sz)[:, None, None]

    def block(args):
        qb, qib, wb, i = args
        qpos = i * Q_BLOCK + jnp.arange(Q_BLOCK)
        qc = lax.dynamic_slice(cid, (i * Q_BLOCK,), (Q_BLOCK,))
        dots = jnp.einsum('bqhd,bkd->bqhk', qib, k_idx).astype(jnp.float32) * idx_scale
        isc = jnp.einsum('bqh,bqhk->bqk', wb.astype(jnp.float32), jax.nn.relu(dots))
        adm = cid[None, :] <= qc[:, None]
        isc = jnp.where(adm[None], isc, -jnp.inf)
        _, sel = lax.top_k(isc, topk)
        kg = k[bidx, sel]
        vg = v[bidx, sel]
        qg = qb.reshape(b_sz, Q_BLOCK, g, r, dh)
        s = jnp.einsum('bqgrd,bqkgd->bgrqk', qg, kg).astype(jnp.float32) * scale
        s = s.reshape(b_sz, n_heads, Q_BLOCK, topk)
        bucket = rel_bucket(sel - qpos[None, :, None])
        s = s + jnp.moveaxis(rel_bias[bucket].astype(jnp.float32), -1, 1)
        ok = cid[sel] <= qc[None, :, None]
        s = jnp.where(ok[:, None], s, NEG)
        p = jax.nn.softmax(s, axis=-1).reshape(b_sz, g, r, Q_BLOCK, topk)
        o = jnp.einsum('bgrqk,bqkgd->bqgrd', p.astype(v.dtype), vg)
        return o.reshape(b_sz, Q_BLOCK, n_heads, dh)

    nb = t_len // Q_BLOCK
    out = lax.map(block, (to_blocks(q), to_blocks(q_idx), to_blocks(w_idx), jnp.arange(nb)))
    return from_blocks(out)


def swiglu(h, w1, w3, w2):
    return (jax.nn.silu(h @ w1) * (h @ w3)) @ w2


def moe_ffn(h, w_router, w1, w3, w2):
    logits = (h @ w_router).astype(jnp.float32)
    top_val, top_idx = lax.top_k(logits, TOP_K_EXPERTS)
    top_w = jax.nn.softmax(top_val, axis=-1)
    gates = jnp.sum(jax.nn.one_hot(top_idx, N_EXPERTS, dtype=jnp.float32) * top_w[..., None], axis=-2)
    out = jnp.zeros_like(h)
    for e in range(N_EXPERTS):
        out = out + gates[..., e:e + 1].astype(h.dtype) * swiglu(h, w1[e], w3[e], w2[e])
    return out


def setup_inputs(seed: int = 0) -> dict:
    key = jax.random.key(seed)
    ks = jax.random.split(key, 32)
    f32 = jnp.float32

    def nrm(k, shape, fan):
        return jax.random.normal(k, shape, f32) * (fan ** -0.5)

    def gain(k, shape):
        return 1.0 + 0.05 * jax.random.normal(k, shape, f32)

    return {
        "x": jax.random.normal(ks[0], (BATCH, SEQ, D_MODEL), f32),
        "meta_tokens": jax.random.normal(ks[1], (N_META, D_MODEL), f32),
        "rel_bias": 0.5 * jax.random.normal(ks[2], (REL_BUCKETS, B_HEADS), f32),
        "norm_mix": gain(ks[3], (DEPTH, D_MODEL)),
        "w_in": nrm(ks[4], (DEPTH, D_MODEL, IN_COLS), D_MODEL),
        "b_forget": 3.0 + 0.5 * jax.random.normal(ks[5], (DEPTH, A_HEADS), f32),
        "q_norm_a": gain(ks[6], (DEPTH, HEAD_DIM)),
        "k_norm_a": gain(ks[7], (DEPTH, HEAD_DIM)),
        "kv_norm_b": gain(ks[8], (DEPTH, B_KV_LATENT)),
        "w_ukv_b": nrm(ks[9], (DEPTH, B_KV_LATENT, 2 * B_KV_HEADS * HEAD_DIM), B_KV_LATENT),
        "q_norm_b": gain(ks[10], (DEPTH, HEAD_DIM)),
        "k_norm_b": gain(ks[11], (DEPTH, HEAD_DIM)),
        "w_proj_a": nrm(ks[12], (DEPTH, A_QW, D_MODEL), A_QW),
        "w_proj_b": nrm(ks[13], (DEPTH, B_QW, D_MODEL), B_QW),
        "w_gate": nrm(ks[14], (DEPTH, D_MODEL, 2 * D_MODEL), D_MODEL),
        "b_gate": 0.01 * jax.random.normal(ks[15], (DEPTH, 2 * D_MODEL), f32),
        "w_out": nrm(ks[16], (DEPTH, D_MODEL, D_MODEL), D_MODEL),
        "norm_ffn": gain(ks[17], (DEPTH, D_MODEL)),
        "dense_w1": nrm(ks[18], (N_DENSE, D_MODEL, D_FF), D_MODEL),
        "dense_w3": nrm(ks[19], (N_DENSE, D_MODEL, D_FF), D_MODEL),
        "dense_w2": nrm(ks[20], (N_DENSE, D_FF, D_MODEL), D_FF),
        "router": nrm(ks[21], (N_MOE, D_MODEL, N_EXPERTS), D_MODEL),
        "moe_w1": nrm(ks[22], (N_MOE, N_EXPERTS, D_MODEL, D_FF_EXPERT), D_MODEL),
        "moe_w3": nrm(ks[23], (N_MOE, N_EXPERTS, D_MODEL, D_FF_EXPERT), D_MODEL),
        "moe_w2": nrm(ks[24], (N_MOE, N_EXPERTS, D_FF_EXPERT, D_MODEL), D_FF_EXPERT),
    }


def reference(x, meta_tokens, rel_bias, norm_mix, w_in, b_forget, q_norm_a, k_norm_a,
              kv_norm_b, w_ukv_b, q_norm_b, k_norm_b, w_proj_a, w_proj_b, w_gate, b_gate,
              w_out, norm_ffn, dense_w1, dense_w3, dense_w2, router, moe_w1, moe_w3, moe_w2):
    b_sz, s_len, d = x.shape
    t_len = N_META + s_len
    t_pad = ((t_len + Q_BLOCK - 1) // Q_BLOCK) * Q_BLOCK
    topk = min(TOPK_MAX, s_len // 4)
    cid = chunk_ids(t_pad)

    meta = jnp.broadcast_to(meta_tokens[None].astype(x.dtype), (b_sz, N_META, d))
    h_res = jnp.concatenate([meta, x], axis=1)
    h_res = jnp.pad(h_res, ((0, 0), (0, t_pad - t_len), (0, 0)))

    for l in range(DEPTH):
        h = rmsnorm(h_res, norm_mix[l])
        proj = h @ w_in[l]
        qa, ka, va, fa, qb, cb, qi, ki, wi = jnp.split(proj, SPLIT_POINTS, axis=-1)

        qa = rmsnorm(qa.reshape(b_sz, t_pad, A_HEADS, HEAD_DIM), q_norm_a[l])
        ka = rmsnorm(ka.reshape(b_sz, t_pad, A_HEADS, HEAD_DIM), k_norm_a[l])
        va = va.reshape(b_sz, t_pad, A_HEADS, HEAD_DIM)
        log_f = jax.nn.log_sigmoid(fa.astype(jnp.float32) + b_forget[l].astype(jnp.float32))
        oa = forgetting_attention(qa, ka, va, log_f).reshape(b_sz, t_pad, A_QW)

        kv = rmsnorm(cb, kv_norm_b[l]) @ w_ukv_b[l]
        kb, vb = jnp.split(kv, 2, axis=-1)
        kb = rmsnorm(kb.reshape(b_sz, t_pad, B_KV_HEADS, HEAD_DIM), k_norm_b[l])
        vb = vb.reshape(b_sz, t_pad, B_KV_HEADS, HEAD_DIM)
        qb = rmsnorm(qb.reshape(b_sz, t_pad, B_HEADS, HEAD_DIM), q_norm_b[l])
        qi = qi.reshape(b_sz, t_pad, IDX_HEADS, IDX_DIM)
        wi = wi * (IDX_HEADS ** -0.5)
        ob = dsa_attention(qb, kb, vb, qi, ki, wi, rel_bias, cid, topk).reshape(b_sz, t_pad, B_QW)

        gate_a, gate_b = jnp.split(jax.nn.sigmoid(h @ w_gate[l] + b_gate[l]), 2, axis=-1)
        merged = gate_a * (oa @ w_proj_a[l]) + gate_b * (ob @ w_proj_b[l])
        h_res = h_res + merged @ w_out[l]

        h2 = rmsnorm(h_res, norm_ffn[l])
        if l % 2 == 0:
            j = l // 2
            ffn = swiglu(h2, dense_w1[j], dense_w3[j], dense_w2[j])
        else:
            j = l // 2
            ffn = moe_ffn(h2, router[j], moe_w1[j], moe_w3[j], moe_w2[j])
        h_res = h_res + ffn

    return h_res[:, N_META:N_META + s_len]
```

```python
import functools

import numpy as np
import jax
import jax.numpy as jnp
from jax import lax
from jax.experimental import pallas as pl
from jax.experimental.pallas import tpu as pltpu

F32 = jnp.float32
BF16 = jnp.bfloat16

CHUNK = 64
N_META = 16
HEAD_DIM = 128
B_KV_HEADS = 2
IDX_HEADS = 16
IDX_DIM = 64
TOPK_MAX = 256
REL_BUCKETS = 32
REL_MAX_DIST = 128
EPS = 1e-6
NEG = -1e30

LANE = 128
VMEM_LIMIT_BYTES = 56 * 1024 * 1024
INT_MIN = -2147483648
NEGINF_KEY = -2139095041


def _pick(n, prefs):
    for p in prefs:
        if n % p == 0:
            return p
    return n


def _params(*sem):
    return pltpu.CompilerParams(dimension_semantics=sem, vmem_limit_bytes=VMEM_LIMIT_BYTES)


def _split3(x):
    hi = x.astype(BF16)
    r1 = x - hi.astype(F32)
    mid = r1.astype(BF16)
    lo = (r1 - mid.astype(F32)).astype(BF16)
    return hi, mid, lo


def _dot(a, b):
    return jnp.dot(a, b, preferred_element_type=F32)


def _dot_nt(a, b):
    return lax.dot_general(a, b, (((1,), (1,)), ((), ())), preferred_element_type=F32)


def _sigmoid(x):
    return 1.0 / (1.0 + jnp.exp(-x))


def _rmsnorm_kernel(x_ref, g_ref, o_ref):
    x = x_ref[...]
    ms = jnp.mean(x * x, axis=-1, keepdims=True)
    o_ref[...] = (x * lax.rsqrt(ms + EPS) * g_ref[...]).astype(o_ref.dtype)


def _rmsnorm(x, g):
    m, d = x.shape
    tm = _pick(m, (256, 128))
    return pl.pallas_call(
        _rmsnorm_kernel,
        grid=(m // tm,),
        in_specs=[pl.BlockSpec((tm, d), lambda i: (i, 0)),
                  pl.BlockSpec((1, d), lambda i: (0, 0))],
        out_specs=pl.BlockSpec((tm, d), lambda i: (i, 0)),
        out_shape=jax.ShapeDtypeStruct((m, d), BF16),
        compiler_params=_params("parallel"),
        name="rmsnorm",
    )(x, g.reshape(1, d))


def _mm_kernel(*refs, n_x, n_w, pair_x, n_extra, nk, epilogue):
    xs = refs[:n_x]
    ws = refs[n_x:n_x + n_w]
    extras = refs[n_x + n_w:n_x + n_w + n_extra]
    o_ref = refs[n_x + n_w + n_extra]
    accs = refs[n_x + n_w + n_extra + 1:]

    def finish(vals):
        o_ref[...] = epilogue(vals, [e[...] for e in extras]).astype(o_ref.dtype)

    if nk == 1:
        finish([_dot(xs[pair_x[p]][...], ws[p][...]) for p in range(n_w)])
        return

    k = pl.program_id(2)
    for p in range(n_w):
        d = _dot(xs[pair_x[p]][...], ws[p][...])

        @pl.when(k == 0)
        def _(d=d, p=p):
            accs[p][...] = d

        @pl.when(k > 0)
        def _(d=d, p=p):
            accs[p][...] += d

    @pl.when(k == nk - 1)
    def _():
        finish([a[...] for a in accs])


def _mm(xs, ws, pair_x, extras, epilogue, out_shape, out_spec, grid, tm, tn, name):
    nk = grid[2]
    arrays = [a for a, _ in xs] + [a for a, _ in ws] + [a for a, _ in extras]
    specs = [s for _, s in xs] + [s for _, s in ws] + [s for _, s in extras]
    scratch = [] if nk == 1 else [pltpu.VMEM((tm, tn), F32) for _ in ws]
    return pl.pallas_call(
        functools.partial(_mm_kernel, n_x=len(xs), n_w=len(ws), pair_x=tuple(pair_x),
                          n_extra=len(extras), nk=nk, epilogue=epilogue),
        grid=grid,
        in_specs=specs,
        out_specs=out_spec,
        out_shape=out_shape,
        scratch_shapes=scratch,
        compiler_params=_params("parallel", "parallel", "arbitrary"),
        name=name,
    )(*arrays)


def _mm_tiles(m, n, k):
    tm = _pick(m, (1408, 1024, 768, 512, 384, 256, 128))
    tn = _pick(n, (1024, 512, 256, 128))
    tk = _pick(k, (1024, 512, 256, 128))
    return tm, tn, tk


def _matmul(x, w, epilogue, extras, out_dtype, name):
    m, kd = x.shape
    n = w.shape[1]
    tm, tn, tk = _mm_tiles(m, n, kd)
    ex = [(a, pl.BlockSpec(bs(tm, tn), im)) for a, bs, im in extras]
    return _mm(
        [(x, pl.BlockSpec((tm, tk), lambda i, j, k: (i, k)))],
        [(w, pl.BlockSpec((tk, tn), lambda i, j, k: (k, j)))],
        [0], ex, epilogue,
        jax.ShapeDtypeStruct((m, n), out_dtype),
        pl.BlockSpec((tm, tn), lambda i, j, k: (i, j)),
        (m // tm, n // tn, kd // tk), tm, tn, name)


def _ep_plain(vals, extras):
    return vals[0]


def _ep_headnorm(vals, extras):
    a, g = vals[0], extras[0]
    outs = []
    for c in range(a.shape[1] // LANE):
        blk = a[:, c * LANE:(c + 1) * LANE]
        ms = jnp.mean(blk * blk, axis=-1, keepdims=True)
        outs.append(blk * lax.rsqrt(ms + EPS) * g[:, c * LANE:(c + 1) * LANE])
    return jnp.concatenate(outs, axis=1)


def _ep_gate(vals, extras):
    return _sigmoid(vals[0] + extras[0])


def _ep_merge(vals, extras):
    return extras[0].astype(F32) * vals[0] + extras[1].astype(F32) * vals[1]


def _ep_residual(vals, extras):
    return extras[0] + vals[0]


def _ep_swiglu(vals, extras):
    a, b = vals
    return a * _sigmoid(a) * b


def _post_kernel(small_ref, cb_ref, brow_ref, kvg_ref, wukv_ref, kng_ref,
                 c_ref, kit_ref, kb_ref, vb_ref, carry_ref, *, tq, n_kv):
    i = pl.program_id(1)
    sm = small_ref[...]
    row = i * tq + lax.broadcasted_iota(jnp.int32, (tq, 1), 0)
    valid = (row < N_META) | (row >= LANE)

    z = sm + brow_ref[...]
    lf = jnp.minimum(z, 0.0) - jnp.log1p(jnp.exp(-jnp.abs(z)))
    lf = jnp.where(valid, lf, 0.0)
    r_i = lax.broadcasted_iota(jnp.int32, (tq, tq), 0)
    c_i = lax.broadcasted_iota(jnp.int32, (tq, tq), 1)
    tri = jnp.where(r_i >= c_i, 1.0, 0.0).astype(BF16)
    hi, mid, lo = _split3(lf)
    cs = _dot(tri, hi) + _dot(tri, mid) + _dot(tri, lo)

    @pl.when(i == 0)
    def _():
        carry_ref[...] = jnp.zeros_like(carry_ref)

    cs = cs + carry_ref[...]
    carry_ref[...] = cs[tq - 1:tq, :]
    cs = jnp.where(valid, cs, -NEG)
    c_ref[...] = cs.T[IDX_DIM:IDX_DIM + c_ref.shape[0], :]

    kit = sm.T[0:IDX_DIM, :].astype(BF16)
    kit_ref[0:IDX_DIM, :] = kit
    kit_ref[IDX_DIM:2 * IDX_DIM, :] = kit

    cbv = cb_ref[...].astype(F32)
    ms = jnp.mean(cbv * cbv, axis=-1, keepdims=True)
    kvn = (cbv * lax.rsqrt(ms + EPS) * kvg_ref[...]).astype(BF16)
    kv = _dot(kvn, wukv_ref[...])
    kw = n_kv * HEAD_DIM
    for g in range(n_kv):
        blk = kv[:, g * HEAD_DIM:(g + 1) * HEAD_DIM]
        msk = jnp.mean(blk * blk, axis=-1, keepdims=True)
        kb_ref[:, g * HEAD_DIM:(g + 1) * HEAD_DIM] = (
            blk * lax.rsqrt(msk + EPS) * kng_ref[...]).astype(BF16)
    vb_ref[...] = kv[:, kw:2 * kw].astype(BF16)


def _post(small, cb, brow, kvg, wukv, kng, b_sz, t_len, n_heads):
    m = small.shape[0]
    latent = cb.shape[1]
    tq = _pick(t_len, (384, 256, 128))
    nq = t_len // tq
    kw = B_KV_HEADS * HEAD_DIM
    rowmap = lambda b, i: (b * nq + i, 0)
    const = lambda b, i: (0, 0)
    return pl.pallas_call(
        functools.partial(_post_kernel, tq=tq, n_kv=B_KV_HEADS),
        grid=(b_sz, nq),
        in_specs=[pl.BlockSpec((tq, LANE), rowmap),
                  pl.BlockSpec((tq, latent), rowmap),
                  pl.BlockSpec((1, LANE), const),
                  pl.BlockSpec((1, latent), const),
                  pl.BlockSpec((latent, 2 * kw), const),
                  pl.BlockSpec((1, HEAD_DIM), const)],
        out_specs=[pl.BlockSpec((None, n_heads, tq), lambda b, i: (b, 0, i)),
                   pl.BlockSpec((None, 2 * IDX_DIM, tq), lambda b, i: (b, 0, i)),
                   pl.BlockSpec((tq, kw), rowmap),
                   pl.BlockSpec((tq, kw), rowmap)],
        out_shape=[jax.ShapeDtypeStruct((b_sz, n_heads, t_len), F32),
                   jax.ShapeDtypeStruct((b_sz, 2 * IDX_DIM, t_len), BF16),
                   jax.ShapeDtypeStruct((m, kw), BF16),
                   jax.ShapeDtypeStruct((m, kw), BF16)],
        scratch_shapes=[pltpu.VMEM((1, LANE), F32)],
        compiler_params=_params("parallel", "arbitrary"),
        name="post_proj",
    )(small, cb, brow, kvg, wukv, kng)


def _fox_kernel(q_ref, k_ref, v_ref, c_ref, o_ref, *, tq):
    i = pl.program_id(2)
    q = q_ref[...]

    def step(j, carry, diagonal):
        m, l, acc = carry
        off = pl.multiple_of(j * tq, tq)
        ks = k_ref[pl.ds(off, tq), :]
        vs = v_ref[pl.ds(off, tq), :]
        s = _dot_nt(q, ks) - c_ref[:, pl.ds(off, tq)]
        if diagonal:
            r_i = lax.broadcasted_iota(jnp.int32, (tq, tq), 0)
            c_i = lax.broadcasted_iota(jnp.int32, (tq, tq), 1)
            s = jnp.where(c_i <= r_i, s, NEG)
        m_new = jnp.maximum(m, jnp.max(s, axis=-1, keepdims=True))
        alpha = jnp.exp(m - m_new)
        p = jnp.exp(s - m_new)
        l = alpha * l + jnp.sum(p, axis=-1, keepdims=True)
        acc = alpha * acc + _dot(p.astype(BF16), vs)
        return m_new, l, acc

    init = (jnp.full((tq, 1), -jnp.inf, F32), jnp.zeros((tq, 1), F32),
            jnp.zeros((tq, HEAD_DIM), F32))
    carry = lax.fori_loop(0, i, lambda j, c: step(j, c, False), init)
    _, l, acc = step(i, carry, True)
    o_ref[...] = (acc / l).astype(o_ref.dtype)


def _fox(qk, v, c, b_sz, t_len, n_heads):
    tq = _pick(t_len, (384, 256, 128))
    nq = t_len // tq
    return pl.pallas_call(
        functools.partial(_fox_kernel, tq=tq),
        grid=(b_sz, n_heads, nq),
        in_specs=[pl.BlockSpec((None, tq, HEAD_DIM), lambda b, h, i: (b, i, h)),
                  pl.BlockSpec((None, t_len, HEAD_DIM), lambda b, h, i: (b, 0, n_heads + h)),
                  pl.BlockSpec((None, t_len, HEAD_DIM), lambda b, h, i: (b, 0, h)),
                  pl.BlockSpec((None, None, 1, t_len), lambda b, h, i: (b, h, 0, 0))],
        out_specs=pl.BlockSpec((None, tq, HEAD_DIM), lambda b, h, i: (b, i, h)),
        out_shape=jax.ShapeDtypeStruct((b_sz, t_len, n_heads * HEAD_DIM), BF16),
        compiler_params=_params("parallel", "parallel", "parallel"),
        name="fox_attention",
    )(qk, qk, v, c)


def _rel_bucket_int(rel):
    nb = REL_BUCKETS // 2
    max_exact = nb // 2
    ret = nb if rel > 0 else 0
    n = abs(rel)
    if n < max_exact:
        return ret + n
    span = nb - max_exact
    k = 0
    while (n ** span) * (max_exact ** (k + 1)) >= (REL_MAX_DIST ** (k + 1)) * (max_exact ** span):
        k += 1
        if max_exact + k >= nb - 1:
            break
    return ret + min(max_exact + k, nb - 1)


def _bucket_strips():
    far = _rel_bucket_int(-(LANE + 1))
    assert far == _rel_bucket_int(-(10 ** 6))
    q = np.arange(LANE)[:, None]
    k = np.arange(LANE)[None, :]
    lut = {r: _rel_bucket_int(r) for r in range(-3 * LANE, 3 * LANE)}
    f = np.vectorize(lambda r: lut[int(r)])
    own = f(k - q)
    prev = f(k - q - LANE)
    meta1 = f(k - (q + N_META))
    farblk = np.full((LANE, LANE), far)
    strips = np.stack([np.concatenate([own, farblk], 1),
                       np.concatenate([meta1, own], 1),
                       np.concatenate([prev, own], 1)])
    return strips.reshape(3, 1, LANE * 2 * LANE).astype(np.int32), far


def _bias_kernel(b_ref, r_ref, o_ref, *, far):
    bk = b_ref[...]
    w = bk.shape[1]
    oh = jnp.where(lax.broadcasted_iota(jnp.int32, (LANE, w), 0) == bk, 1.0, 0.0).astype(BF16)
    rb = r_ref[...]
    rb = rb - rb[:, far:far + 1]
    hi, mid, lo = _split3(rb)
    o_ref[...] = _dot(hi, oh) + _dot(mid, oh) + _dot(lo, oh)


def _bias_strips(rel_bias):
    n_heads = rel_bias.shape[1]
    hp = max(16, n_heads)
    strips, far = _bucket_strips()
    w_all = strips.shape[2]
    wc = 4096
    rbt = jnp.zeros((hp, LANE), F32).at[:n_heads, :REL_BUCKETS].set(rel_bias.T.astype(F32))
    out = pl.pallas_call(
        functools.partial(_bias_kernel, far=far),
        grid=(3, w_all // wc),
        in_specs=[pl.BlockSpec((None, 1, wc), lambda s, c: (s, 0, c)),
                  pl.BlockSpec((hp, LANE), lambda s, c: (0, 0))],
        out_specs=pl.BlockSpec((None, hp, wc), lambda s, c: (s, 0, c)),
        out_shape=jax.ShapeDtypeStruct((3, hp, w_all), F32),
        compiler_params=_params("parallel", "parallel"),
        name="rel_bias_strips",
    )(jnp.asarray(strips), rbt)
    return out[:, :n_heads].reshape(3, n_heads, LANE, 2 * LANE)


def _dsa_kernel(qi_ref, small_ref, kit_ref, qb_ref, kb_ref, vb_ref, bias_ref, o_ref,
                key_sc, *, kc, topk, n_heads):
    j = pl.program_id(1)
    bpc = kc // LANE
    nch = j // bpc + 1
    rep = n_heads // B_KV_HEADS

    lane = lax.broadcasted_iota(jnp.int32, (LANE, LANE), 1)
    lo_m = jnp.where(lane < IDX_DIM, 1.0, 0.0).astype(BF16)
    hi_m = jnp.where(lane >= IDX_DIM, 1.0, 0.0).astype(BF16)
    parts = []
    for p in range(IDX_HEADS // 2):
        blk = qi_ref[:, p * LANE:(p + 1) * LANE]
        parts.append(blk * lo_m)
        parts.append(blk * hi_m)
    lhs = jnp.concatenate(parts, axis=0)
    w_off = IDX_DIM + IDX_HEADS
    w = small_ref[:, w_off:w_off + IDX_HEADS] * ((IDX_HEADS ** -0.5) * (IDX_DIM ** -0.5))
    qrow = j * LANE + lax.broadcasted_iota(jnp.int32, (LANE, kc), 0)

    def idx_body(c, _):
        off = pl.multiple_of(c * kc, kc)
        d = _dot(lhs, kit_ref[:, pl.ds(off, kc)])
        acc = jnp.zeros((LANE, kc), F32)
        for h in range(IDX_HEADS):
            acc = acc + w[:, h:h + 1] * jnp.maximum(d[h * LANE:(h + 1) * LANE], 0.0)
        krow = off + lax.broadcasted_iota(jnp.int32, (LANE, kc), 1)
        adm = (krow < N_META) | ((krow >= LANE) & ((krow // CHUNK) <= (qrow // CHUNK)))
        isc = jnp.where(adm, acc, -jnp.inf)
        bits = pltpu.bitcast(isc, jnp.int32)
        key_sc[:, pl.ds(off, kc)] = bits ^ ((bits >> 31) & 0x7FFFFFFF)
        return 0

    lax.fori_loop(0, nch, idx_body, 0)

    def search(b, t_u):
        bit = jnp.left_shift(jnp.int32(1), 31 - b)
        cand_u = t_u | bit
        cand_s = cand_u ^ INT_MIN

        def cnt_body(c, cnt):
            off = pl.multiple_of(c * kc, kc)
            ge = jnp.where(key_sc[:, pl.ds(off, kc)] >= cand_s, 1, 0)
            for t in range(bpc):
                cnt = cnt + ge[:, t * LANE:(t + 1) * LANE]
            return cnt

        cnt = lax.fori_loop(0, nch, cnt_body, jnp.zeros((LANE, LANE), jnp.int32))
        tot = jnp.sum(cnt, axis=-1, keepdims=True)
        return jnp.where(tot >= topk, cand_u, t_u)

    t_u = lax.fori_loop(0, 32, search, jnp.zeros((LANE, 1), jnp.int32))
    thr = jnp.maximum(t_u ^ INT_MIN, NEGINF_KEY + 1)

    near = pl.multiple_of(jnp.maximum(j - 1, 0) * LANE, LANE)
    nfar = jnp.maximum(j - 2 + bpc, 0) // bpc
    far_limit = (j - 1) * LANE

    for g in range(B_KV_HEADS):
        qg = jnp.concatenate(
            [qb_ref[:, (g * rep + r) * HEAD_DIM:(g * rep + r + 1) * HEAD_DIM] for r in range(rep)],
            axis=0)
        gsl = slice(g * HEAD_DIM, (g + 1) * HEAD_DIM)

        def attend(off, width, bias, limit, carry):
            m, l, acc = carry
            kk = kb_ref[pl.ds(off, width), gsl]
            vv = vb_ref[pl.ds(off, width), gsl]
            s = _dot_nt(qg, kk).reshape(rep, LANE, width)
            sel = key_sc[:, pl.ds(off, width)] >= thr
            if limit is not None:
                krow = off + lax.broadcasted_iota(jnp.int32, (LANE, width), 1)
                sel = sel & (krow < limit)
            if bias is not None:
                s = s + bias
            s = jnp.where(sel[None], s, NEG)
            m_new = jnp.maximum(m, jnp.max(s, axis=-1, keepdims=True))
            alpha = jnp.exp(m - m_new)
            p = jnp.exp(s - m_new)
            l = alpha * l + jnp.sum(p, axis=-1, keepdims=True)
            pv = _dot(p.reshape(rep * LANE, width).astype(BF16), vv)
            acc = alpha * acc + pv.reshape(rep, LANE, HEAD_DIM)
            return m_new, l, acc

        init = (jnp.full((rep, LANE, 1), -jnp.inf, F32), jnp.zeros((rep, LANE, 1), F32),
                jnp.zeros((rep, LANE, HEAD_DIM), F32))
        carry = attend(near, 2 * LANE, bias_ref[g * rep:(g + 1) * rep], None, init)

        def far_body(c, carry):
            return attend(pl.multiple_of(c * kc, kc), kc, None, far_limit, carry)

        _, l, acc = lax.fori_loop(0, nfar, far_body, carry)
        out = acc / l
        for r in range(rep):
            o_ref[:, (g * rep + r) * HEAD_DIM:(g * rep + r + 1) * HEAD_DIM] = out[r].astype(o_ref.dtype)


def _dsa(qi, small, kit, qk, kb, vb, bias, b_sz, t_len, n_heads, topk):
    nb = t_len // LANE
    kc = _pick(t_len, (384, 256))
    assert kc >= topk and t_len >= 2 * LANE
    hw = n_heads * HEAD_DIM
    kw = B_KV_HEADS * HEAD_DIM
    rowmap = lambda b, j: (b * nb + j, 0)
    return pl.pallas_call(
        functools.partial(_dsa_kernel, kc=kc, topk=topk, n_heads=n_heads),
        grid=(b_sz, nb),
        in_specs=[pl.BlockSpec((LANE, IDX_HEADS * IDX_DIM), rowmap),
                  pl.BlockSpec((LANE, LANE), rowmap),
                  pl.BlockSpec((None, 2 * IDX_DIM, t_len), lambda b, j: (b, 0, 0)),
                  pl.BlockSpec((LANE, hw), lambda b, j: (b * nb + j, 2)),
                  pl.BlockSpec((None, t_len, kw), lambda b, j: (b, 0, 0)),
                  pl.BlockSpec((None, t_len, kw), lambda b, j: (b, 0, 0)),
                  pl.BlockSpec((None, n_heads, LANE, 2 * LANE),
                               lambda b, j: (jnp.minimum(j, 2), 0, 0, 0))],
        out_specs=pl.BlockSpec((LANE, hw), rowmap),
        out_shape=jax.ShapeDtypeStruct((qi.shape[0], hw), BF16),
        scratch_shapes=[pltpu.VMEM((LANE, t_len), jnp.int32)],
        compiler_params=_params("parallel", "parallel"),
        name="dsa_attention",
    )(qi, small, kit, qk, kb, vb, bias)


def _norm_router_kernel(x_ref, g_ref, r_ref, o_ref, gate_ref, *, n_exp):
    x = x_ref[...]
    ms = jnp.mean(x * x, axis=-1, keepdims=True)
    y = x * lax.rsqrt(ms + EPS) * g_ref[...]
    o_ref[...] = y.astype(o_ref.dtype)
    logits = [jnp.sum(y * r_ref[e:e + 1, :], axis=-1, keepdims=True) for e in range(n_exp)]

    def argmax(vals):
        best, bi = vals[0], jnp.zeros_like(vals[0], dtype=jnp.int32)
        for e in range(1, n_exp):
            upd = vals[e] > best
            best = jnp.where(upd, vals[e], best)
            bi = jnp.where(upd, e, bi)
        return best, bi

    m1, i1 = argmax(logits)
    m2, i2 = argmax([jnp.where(i1 == e, -jnp.inf, logits[e]) for e in range(n_exp)])
    t = jnp.exp(m2 - m1)
    w1 = 1.0 / (1.0 + t)
    w2 = t / (1.0 + t)
    lane = lax.broadcasted_iota(jnp.int32, gate_ref.shape, 1)
    gate_ref[...] = jnp.where(lane == i1, w1, jnp.where(lane == i2, w2, 0.0))


def _norm_router(x, g, router):
    m, d = x.shape
    n_exp = router.shape[1]
    tm = _pick(m, (256, 128))
    return pl.pallas_call(
        functools.partial(_norm_router_kernel, n_exp=n_exp),
        grid=(m // tm,),
        in_specs=[pl.BlockSpec((tm, d), lambda i: (i, 0)),
                  pl.BlockSpec((1, d), lambda i: (0, 0)),
                  pl.BlockSpec((n_exp, d), lambda i: (0, 0))],
        out_specs=[pl.BlockSpec((tm, d), lambda i: (i, 0)),
                   pl.BlockSpec((tm, LANE), lambda i: (i, 0))],
        out_shape=[jax.ShapeDtypeStruct((m, d), BF16),
                   jax.ShapeDtypeStruct((m, LANE), F32)],
        compiler_params=_params("parallel"),
        name="norm_router",
    )(x, g.reshape(1, d), router.T.astype(F32))


def _dense_ffn(h2, h_res, w1, w3, w2):
    m, d = h2.shape
    f = w1.shape[1]
    fp = ((f + 1023) // 1024) * 1024 if f > 1024 else f
    pad = ((0, 0), (0, fp - f))
    w1p = jnp.pad(w1, pad).astype(BF16)
    w3p = jnp.pad(w3, pad).astype(BF16)
    w2p = jnp.pad(w2, ((0, fp - f), (0, 0))).astype(BF16)
    tm, tn, tk = _mm_tiles(m, fp, d)
    wspec = pl.BlockSpec((tk, tn), lambda i, j, k: (k, j))
    a = _mm([(h2, pl.BlockSpec((tm, tk), lambda i, j, k: (i, k)))],
            [(w1p, wspec), (w3p, wspec)], [0, 0], [], _ep_swiglu,
            jax.ShapeDtypeStruct((m, fp), BF16),
            pl.BlockSpec((tm, tn), lambda i, j, k: (i, j)),
            (m // tm, fp // tn, d // tk), tm, tn, "ffn_up")
    return _matmul(a, w2p, _ep_residual,
                   [(h_res, lambda tm, tn: (tm, tn), lambda i, j, k: (i, j))], F32, "ffn_down")


def _moe_ffn(h2, gates, h_res, w1, w3, w2):
    m, d = h2.shape
    n_exp, _, f = w1.shape
    tm, tn, tk = _mm_tiles(m, f, d)
    nj = f // tn

    def ep(vals, extras):
        a, b = vals
        e = pl.program_id(1) // nj
        lane = lax.broadcasted_iota(jnp.int32, extras[0].shape, 1)
        gate = jnp.sum(jnp.where(lane == e, extras[0], 0.0), axis=-1, keepdims=True)
        return a * _sigmoid(a) * b * gate

    wspec = pl.BlockSpec((None, tk, tn), lambda i, j, k: (j // nj, k, j % nj))
    a = _mm([(h2, pl.BlockSpec((tm, tk), lambda i, j, k: (i, k)))],
            [(w1.astype(BF16), wspec), (w3.astype(BF16), wspec)], [0, 0],
            [(gates, pl.BlockSpec((tm, LANE), lambda i, j, k: (i, 0)))], ep,
            jax.ShapeDtypeStruct((m, n_exp * f), BF16),
            pl.BlockSpec((tm, tn), lambda i, j, k: (i, j)),
            (m // tm, n_exp * nj, d // tk), tm, tn, "moe_up")
    return _matmul(a, w2.astype(BF16).reshape(n_exp * f, d), _ep_residual,
                   [(h_res, lambda tm, tn: (tm, tn), lambda i, j, k: (i, j))], F32, "moe_down")


def kernel(x, meta_tokens, rel_bias, norm_mix, w_in, b_forget, q_norm_a, k_norm_a, kv_norm_b, w_ukv_b, q_norm_b, k_norm_b, w_proj_a, w_proj_b, w_gate, b_gate, w_out, norm_ffn, dense_w1, dense_w3, dense_w2, router, moe_w1, moe_w3, moe_w2):
    b_sz, s_len, d = x.shape
    depth = norm_mix.shape[0]
    n_heads = d // (2 * HEAD_DIM)
    hw = n_heads * HEAD_DIM
    latent = kv_norm_b.shape[1]
    iqw = IDX_HEADS * IDX_DIM
    assert s_len % LANE == 0 and N_META <= LANE and n_heads % B_KV_HEADS == 0
    t_len = LANE + s_len
    m = b_sz * t_len
    topk = min(TOPK_MAX, s_len // 4)
    scale = HEAD_DIM ** -0.5

    sizes = (hw, hw, hw, n_heads, hw, latent, iqw, IDX_DIM, IDX_HEADS)
    offs = np.concatenate([[0], np.cumsum(sizes)])
    o_qa, o_ka, o_va, o_fa, o_qb, o_cb, o_qi, o_ki, o_wi = (int(v) for v in offs[:-1])

    meta = jnp.broadcast_to(meta_tokens[None].astype(F32), (b_sz, N_META, d))
    h_res = jnp.concatenate(
        [meta, jnp.zeros((b_sz, LANE - N_META, d), F32), x.astype(F32)], axis=1).reshape(m, d)

    bias = _bias_strips(rel_bias)

    for l in range(depth):
        w = w_in[l]
        w_qkq = jnp.concatenate([w[:, o_qa:o_qa + hw], w[:, o_ka:o_ka + hw], w[:, o_qb:o_qb + hw]],
                                axis=1).astype(BF16)
        w_va = w[:, o_va:o_va + hw].astype(BF16)
        w_qi = w[:, o_qi:o_qi + iqw].astype(BF16)
        w_cb = w[:, o_cb:o_cb + latent].astype(BF16)
        n_small = IDX_DIM + n_heads + IDX_HEADS
        assert n_small <= LANE and IDX_DIM + n_heads <= IDX_DIM + IDX_HEADS
        w_small = jnp.concatenate(
            [w[:, o_ki:o_ki + IDX_DIM], w[:, o_fa:o_fa + n_heads],
             jnp.zeros((d, IDX_HEADS - n_heads), F32), w[:, o_wi:o_wi + IDX_HEADS],
             jnp.zeros((d, LANE - IDX_DIM - 2 * IDX_HEADS), F32)], axis=1).astype(BF16)
        gains = jnp.concatenate([jnp.tile(q_norm_a[l] * scale, n_heads), jnp.tile(k_norm_a[l], n_heads),
                                 jnp.tile(q_norm_b[l] * scale, n_heads)]).reshape(1, 3 * hw).astype(F32)
        brow = jnp.zeros((1, LANE), F32).at[0, IDX_DIM:IDX_DIM + n_heads].set(b_forget[l].astype(F32))

        h = _rmsnorm(h_res, norm_mix[l])
        qkq = _matmul(h, w_qkq, _ep_headnorm,
                      [(gains, lambda tm, tn: (1, tn), lambda i, j, k: (0, j))], BF16, "proj_qkq")
        va = _matmul(h, w_va, _ep_plain, [], BF16, "proj_va")
        qi = _matmul(h, w_qi, _ep_plain, [], BF16, "proj_qi")
        cb = _matmul(h, w_cb, _ep_plain, [], BF16, "proj_cb")
        small = _matmul(h, w_small, _ep_plain, [], F32, "proj_small")
        gate = _matmul(h, w_gate[l].astype(BF16), _ep_gate,
                       [(b_gate[l].reshape(1, 2 * d).astype(F32), lambda tm, tn: (1, tn),
                         lambda i, j, k: (0, j))], BF16, "proj_gate")

        c, kit, kb, vb = _post(small, cb, brow, kv_norm_b[l].reshape(1, latent).astype(F32),
                               w_ukv_b[l].astype(BF16), k_norm_b[l].reshape(1, HEAD_DIM).astype(F32),
                               b_sz, t_len, n_heads)

        oa = _fox(qkq.reshape(b_sz, t_len, 3 * hw), va.reshape(b_sz, t_len, hw),
                  c.reshape(b_sz, n_heads, 1, t_len), b_sz, t_len, n_heads).reshape(m, hw)
        kw = B_KV_HEADS * HEAD_DIM
        ob = _dsa(qi, small, kit, qkq, kb.reshape(b_sz, t_len, kw), vb.reshape(b_sz, t_len, kw),
                  bias, b_sz, t_len, n_heads, topk)

        tm, tn, tk = _mm_tiles(m, d, hw)
        nj = d // tn
        xspec = pl.BlockSpec((tm, tk), lambda i, j, k: (i, k))
        wspec = pl.BlockSpec((tk, tn), lambda i, j, k: (k, j))
        merged = _mm([(oa, xspec), (ob, xspec)],
                     [(w_proj_a[l].astype(BF16), wspec), (w_proj_b[l].astype(BF16), wspec)], [0, 1],
                     [(gate, pl.BlockSpec((tm, tn), lambda i, j, k: (i, j))),
                      (gate, pl.BlockSpec((tm, tn), lambda i, j, k: (i, j + nj)))],
                     _ep_merge, jax.ShapeDtypeStruct((m, d), BF16),
                     pl.BlockSpec((tm, tn), lambda i, j, k: (i, j)),
                     (m // tm, nj, hw // tk), tm, tn, "merge_proj")
        h_res = _matmul(merged, w_out[l].astype(BF16), _ep_residual,
                        [(h_res, lambda tm, tn: (tm, tn), lambda i, j, k: (i, j))], F32, "out_proj")

        jj = l // 2
        if l % 2 == 0:
            h2 = _rmsnorm(h_res, norm_ffn[l])
            h_res = _dense_ffn(h2, h_res, dense_w1[jj], dense_w3[jj], dense_w2[jj])
        else:
            h2, gates = _norm_router(h_res, norm_ffn[l], router[jj])
            h_res = _moe_ffn(h2, gates, h_res, moe_w1[jj], moe_w3[jj], moe_w2[jj])

    return h_res.reshape(b_sz, t_len, d)[:, LANE:, :]
```

```python
import functools

import numpy as np
import jax
import jax.numpy as jnp
from jax import lax
from jax.experimental import pallas as pl
from jax.experimental.pallas import tpu as pltpu

F32 = jnp.float32
BF16 = jnp.bfloat16

CHUNK = 64
N_META = 16
HEAD_DIM = 128
B_KV_HEADS = 2
IDX_HEADS = 16
IDX_DIM = 64
TOPK_MAX = 256
REL_BUCKETS = 32
REL_MAX_DIST = 128
EPS = 1e-6
NEG = -1e30

LANE = 128
VMEM_LIMIT_BYTES = 56 * 1024 * 1024
INT_MIN = -2147483648
NEGINF_KEY = -2139095041


def _pick(n, prefs):
    for p in prefs:
        if n % p == 0:
            return p
    return n


def _params(*sem):
    return pltpu.CompilerParams(dimension_semantics=sem, vmem_limit_bytes=VMEM_LIMIT_BYTES)


def _split3(x):
    hi = x.astype(BF16)
    r1 = x - hi.astype(F32)
    mid = r1.astype(BF16)
    lo = (r1 - mid.astype(F32)).astype(BF16)
    return hi, mid, lo


def _dot(a, b):
    return jnp.dot(a, b, preferred_element_type=F32)


def _dot_nt(a, b):
    return lax.dot_general(a, b, (((1,), (1,)), ((), ())), preferred_element_type=F32)


def _sigmoid(x):
    return 1.0 / (1.0 + jnp.exp(-x))


def _rmsnorm_kernel(x_ref, g_ref, o_ref):
    x = x_ref[...]
    ms = jnp.mean(x * x, axis=-1, keepdims=True)
    o_ref[...] = (x * lax.rsqrt(ms + EPS) * g_ref[...]).astype(o_ref.dtype)


def _rmsnorm(x, g):
    m, d = x.shape
    tm = _pick(m, (256, 128))
    return pl.pallas_call(
        _rmsnorm_kernel,
        grid=(m // tm,),
        in_specs=[pl.BlockSpec((tm, d), lambda i: (i, 0)),
                  pl.BlockSpec((1, d), lambda i: (0, 0))],
        out_specs=pl.BlockSpec((tm, d), lambda i: (i, 0)),
        out_shape=jax.ShapeDtypeStruct((m, d), BF16),
        compiler_params=_params("parallel"),
        name="rmsnorm",
    )(x, g.reshape(1, d))


def _mm_kernel(*refs, n_x, n_w, pair_x, n_extra, nk, epilogue):
    xs = refs[:n_x]
    ws = refs[n_x:n_x + n_w]
    extras = refs[n_x + n_w:n_x + n_w + n_extra]
    o_ref = refs[n_x + n_w + n_extra]
    accs = refs[n_x + n_w + n_extra + 1:]

    def finish(vals):
        o_ref[...] = epilogue(vals, [e[...] for e in extras]).astype(o_ref.dtype)

    if nk == 1:
        finish([_dot(xs[pair_x[p]][...], ws[p][...]) for p in range(n_w)])
        return

    k = pl.program_id(2)
    for p in range(n_w):
        d = _dot(xs[pair_x[p]][...], ws[p][...])

        @pl.when(k == 0)
        def _(d=d, p=p):
            accs[p][...] = d

        @pl.when(k > 0)
        def _(d=d, p=p):
            accs[p][...] += d

    @pl.when(k == nk - 1)
    def _():
        finish([a[...] for a in accs])


def _mm(xs, ws, pair_x, extras, epilogue, out_shape, out_spec, grid, tm, tn, name):
    nk = grid[2]
    arrays = [a for a, _ in xs] + [a for a, _ in ws] + [a for a, _ in extras]
    specs = [s for _, s in xs] + [s for _, s in ws] + [s for _, s in extras]
    scratch = [] if nk == 1 else [pltpu.VMEM((tm, tn), F32) for _ in ws]
    return pl.pallas_call(
        functools.partial(_mm_kernel, n_x=len(xs), n_w=len(ws), pair_x=tuple(pair_x),
                          n_extra=len(extras), nk=nk, epilogue=epilogue),
        grid=grid,
        in_specs=specs,
        out_specs=out_spec,
        out_shape=out_shape,
        scratch_shapes=scratch,
        compiler_params=_params("parallel", "parallel", "arbitrary"),
        name=name,
    )(*arrays)


def _mm_tiles(m, n, k):
    tm = _pick(m, (1408, 1024, 768, 512, 384, 256, 128))
    tn = _pick(n, (1024, 512, 256, 128))
    tk = _pick(k, (1024, 512, 256, 128))
    return tm, tn, tk


def _matmul(x, w, epilogue, extras, out_dtype, name):
    m, kd = x.shape
    n = w.shape[1]
    tm, tn, tk = _mm_tiles(m, n, kd)
    ex = [(a, pl.BlockSpec(bs(tm, tn), im)) for a, bs, im in extras]
    return _mm(
        [(x, pl.BlockSpec((tm, tk), lambda i, j, k: (i, k)))],
        [(w, pl.BlockSpec((tk, tn), lambda i, j, k: (k, j)))],
        [0], ex, epilogue,
        jax.ShapeDtypeStruct((m, n), out_dtype),
        pl.BlockSpec((tm, tn), lambda i, j, k: (i, j)),
        (m // tm, n // tn, kd // tk), tm, tn, name)


def _ep_plain(vals, extras):
    return vals[0]


def _ep_headnorm(vals, extras):
    a, g = vals[0], extras[0]
    outs = []
    for c in range(a.shape[1] // LANE):
        blk = a[:, c * LANE:(c + 1) * LANE]
        ms = jnp.mean(blk * blk, axis=-1, keepdims=True)
        outs.append(blk * lax.rsqrt(ms + EPS) * g[:, c * LANE:(c + 1) * LANE])
    return jnp.concatenate(outs, axis=1)


def _ep_gate(vals, extras):
    return _sigmoid(vals[0] + extras[0])


def _ep_merge(vals, extras):
    return extras[0].astype(F32) * vals[0] + extras[1].astype(F32) * vals[1]


def _ep_residual(vals, extras):
    return extras[0] + vals[0]


def _ep_swiglu(vals, extras):
    a, b = vals
    return a * _sigmoid(a) * b


def _post_kernel(small_ref, cb_ref, brow_ref, kvg_ref, wukv_ref, kng_ref,
                 c_ref, kit_ref, kb_ref, vb_ref, carry_ref, *, tq, n_kv):
    i = pl.program_id(1)
    sm = small_ref[...]
    row = i * tq + lax.broadcasted_iota(jnp.int32, (tq, 1), 0)
    valid = (row < N_META) | (row >= LANE)

    z = sm + brow_ref[...]
    lf = jnp.minimum(z, 0.0) - jnp.log1p(jnp.exp(-jnp.abs(z)))
    lf = jnp.where(valid, lf, 0.0)
    r_i = lax.broadcasted_iota(jnp.int32, (tq, tq), 0)
    c_i = lax.broadcasted_iota(jnp.int32, (tq, tq), 1)
    tri = jnp.where(r_i >= c_i, 1.0, 0.0).astype(BF16)
    hi, mid, lo = _split3(lf)
    cs = _dot(tri, hi) + _dot(tri, mid) + _dot(tri, lo)

    @pl.when(i == 0)
    def _():
        carry_ref[...] = jnp.zeros_like(carry_ref)

    cs = cs + carry_ref[...]
    carry_ref[...] = cs[tq - 1:tq, :]
    cs = jnp.where(valid, cs, -NEG)
    c_ref[...] = cs.T[IDX_DIM:IDX_DIM + c_ref.shape[0], :]

    kit = sm.T[0:IDX_DIM, :].astype(BF16)
    kit_ref[0:IDX_DIM, :] = kit
    kit_ref[IDX_DIM:2 * IDX_DIM, :] = kit

    cbv = cb_ref[...].astype(F32)
    ms = jnp.mean(cbv * cbv, axis=-1, keepdims=True)
    kvn = (cbv * lax.rsqrt(ms + EPS) * kvg_ref[...]).astype(BF16)
    kv = _dot(kvn, wukv_ref[...])
    kw = n_kv * HEAD_DIM
    for g in range(n_kv):
        blk = kv[:, g * HEAD_DIM:(g + 1) * HEAD_DIM]
        msk = jnp.mean(blk * blk, axis=-1, keepdims=True)
        kb_ref[:, g * HEAD_DIM:(g + 1) * HEAD_DIM] = (
            blk * lax.rsqrt(msk + EPS) * kng_ref[...]).astype(BF16)
    vb_ref[...] = kv[:, kw:2 * kw].astype(BF16)


def _post(small, cb, brow, kvg, wukv, kng, b_sz, t_len, n_heads):
    m = small.shape[0]
    latent = cb.shape[1]
    tq = _pick(t_len, (384, 256, 128))
    nq = t_len // tq
    kw = B_KV_HEADS * HEAD_DIM
    rowmap = lambda b, i: (b * nq + i, 0)
    const = lambda b, i: (0, 0)
    return pl.pallas_call(
        functools.partial(_post_kernel, tq=tq, n_kv=B_KV_HEADS),
        grid=(b_sz, nq),
        in_specs=[pl.BlockSpec((tq, LANE), rowmap),
                  pl.BlockSpec((tq, latent), rowmap),
                  pl.BlockSpec((1, LANE), const),
                  pl.BlockSpec((1, latent), const),
                  pl.BlockSpec((latent, 2 * kw), const),
                  pl.BlockSpec((1, HEAD_DIM), const)],
        out_specs=[pl.BlockSpec((None, n_heads, tq), lambda b, i: (b, 0, i)),
                   pl.BlockSpec((None, 2 * IDX_DIM, tq), lambda b, i: (b, 0, i)),
                   pl.BlockSpec((tq, kw), rowmap),
                   pl.BlockSpec((tq, kw), rowmap)],
        out_shape=[jax.ShapeDtypeStruct((b_sz, n_heads, t_len), F32),
                   jax.ShapeDtypeStruct((b_sz, 2 * IDX_DIM, t_len), BF16),
                   jax.ShapeDtypeStruct((m, kw), BF16),
                   jax.ShapeDtypeStruct((m, kw), BF16)],
        scratch_shapes=[pltpu.VMEM((1, LANE), F32)],
        compiler_params=_params("parallel", "arbitrary"),
        name="post_proj",
    )(small, cb, brow, kvg, wukv, kng)


def _fox_kernel(q_ref, k_ref, v_ref, c_ref, o_ref, *, tq):
    i = pl.program_id(2)
    q = q_ref[...]

    def step(j, carry, diagonal):
        m, l, acc = carry
        off = pl.multiple_of(j * tq, tq)
        ks = k_ref[pl.ds(off, tq), :]
        vs = v_ref[pl.ds(off, tq), :]
        s = _dot_nt(q, ks) - c_ref[:, pl.ds(off, tq)]
        if diagonal:
            r_i = lax.broadcasted_iota(jnp.int32, (tq, tq), 0)
            c_i = lax.broadcasted_iota(jnp.int32, (tq, tq), 1)
            s = jnp.where(c_i <= r_i, s, NEG)
        m_new = jnp.maximum(m, jnp.max(s, axis=-1, keepdims=True))
        alpha = jnp.exp(m - m_new)
        p = jnp.exp(s - m_new)
        l = alpha * l + jnp.sum(p, axis=-1, keepdims=True)
        acc = alpha * acc + _dot(p.astype(BF16), vs)
        return m_new, l, acc

    init = (jnp.full((tq, 1), -jnp.inf, F32), jnp.zeros((tq, 1), F32),
            jnp.zeros((tq, HEAD_DIM), F32))
    carry = lax.fori_loop(0, i, lambda j, c: step(j, c, False), init)
    _, l, acc = step(i, carry, True)
    o_ref[...] = (acc / l).astype(o_ref.dtype)


def _fox(qk, v, c, b_sz, t_len, n_heads):
    tq = _pick(t_len, (384, 256, 128))
    nq = t_len // tq
    return pl.pallas_call(
        functools.partial(_fox_kernel, tq=tq),
        grid=(b_sz, n_heads, nq),
        in_specs=[pl.BlockSpec((None, tq, HEAD_DIM), lambda b, h, i: (b, i, h)),
                  pl.BlockSpec((None, t_len, HEAD_DIM), lambda b, h, i: (b, 0, n_heads + h)),
                  pl.BlockSpec((None, t_len, HEAD_DIM), lambda b, h, i: (b, 0, h)),
                  pl.BlockSpec((None, None, 1, t_len), lambda b, h, i: (b, h, 0, 0))],
        out_specs=pl.BlockSpec((None, tq, HEAD_DIM), lambda b, h, i: (b, i, h)),
        out_shape=jax.ShapeDtypeStruct((b_sz, t_len, n_heads * HEAD_DIM), BF16),
        compiler_params=_params("parallel", "parallel", "parallel"),
        name="fox_attention",
    )(qk, qk, v, c)


def _rel_bucket_int(rel):
    nb = REL_BUCKETS // 2
    max_exact = nb // 2
    ret = nb if rel > 0 else 0
    n = abs(rel)
    if n < max_exact:
        return ret + n
    span = nb - max_exact
    k = 0
    while (n ** span) * (max_exact ** (k + 1)) >= (REL_MAX_DIST ** (k + 1)) * (max_exact ** span):
        k += 1
        if max_exact + k >= nb - 1:
            break
    return ret + min(max_exact + k, nb - 1)


def _bucket_strips():
    far = _rel_bucket_int(-(LANE + 1))
    assert far == _rel_bucket_int(-(10 ** 6))
    q = np.arange(LANE)[:, None]
    k = np.arange(LANE)[None, :]
    lut = {r: _rel_bucket_int(r) for r in range(-3 * LANE, 3 * LANE)}
    f = np.vectorize(lambda r: lut[int(r)])
    own = f(k - q)
    prev = f(k - q - LANE)
    meta1 = f(k - (q + N_META))
    farblk = np.full((LANE, LANE), far)
    strips = np.stack([np.concatenate([own, farblk], 1),
                       np.concatenate([meta1, own], 1),
                       np.concatenate([prev, own], 1)])
    return strips.reshape(3, 1, LANE * 2 * LANE).astype(np.int32), far


def _bias_kernel(b_ref, r_ref, o_ref, *, far):
    bk = b_ref[...]
    w = bk.shape[1]
    oh = jnp.where(lax.broadcasted_iota(jnp.int32, (LANE, w), 0) == bk, 1.0, 0.0).astype(BF16)
    rb = r_ref[...]
    rb = rb - rb[:, far:far + 1]
    hi, mid, lo = _split3(rb)
    o_ref[...] = _dot(hi, oh) + _dot(mid, oh) + _dot(lo, oh)


def _bias_strips(rel_bias):
    n_heads = rel_bias.shape[1]
    hp = max(16, n_heads)
    strips, far = _bucket_strips()
    w_all = strips.shape[2]
    wc = 4096
    rbt = jnp.zeros((hp, LANE), F32).at[:n_heads, :REL_BUCKETS].set(rel_bias.T.astype(F32))
    out = pl.pallas_call(
        functools.partial(_bias_kernel, far=far),
        grid=(3, w_all // wc),
        in_specs=[pl.BlockSpec((None, 1, wc), lambda s, c: (s, 0, c)),
                  pl.BlockSpec((hp, LANE), lambda s, c: (0, 0))],
        out_specs=pl.BlockSpec((None, hp, wc), lambda s, c: (s, 0, c)),
        out_shape=jax.ShapeDtypeStruct((3, hp, w_all), F32),
        compiler_params=_params("parallel", "parallel"),
        name="rel_bias_strips",
    )(jnp.asarray(strips), rbt)
    return out[:, :n_heads].reshape(3, n_heads, LANE, 2 * LANE)


def _dsa_kernel(qi_ref, small_ref, kit_ref, qb_ref, kb_ref, vb_ref, bias_ref, o_ref,
                key_sc, *, kc, topk, n_heads):
    j = pl.program_id(1)
    bpc = kc // LANE
    nch = j // bpc + 1
    rep = n_heads // B_KV_HEADS

    lane = lax.broadcasted_iota(jnp.int32, (LANE, LANE), 1)
    lo_m = jnp.where(lane < IDX_DIM, 1.0, 0.0).astype(BF16)
    hi_m = jnp.where(lane >= IDX_DIM, 1.0, 0.0).astype(BF16)
    parts = []
    for p in range(IDX_HEADS // 2):
        blk = qi_ref[:, p * LANE:(p + 1) * LANE]
        parts.append(blk * lo_m)
        parts.append(blk * hi_m)
    lhs = jnp.concatenate(parts, axis=0)
    w_off = IDX_DIM + IDX_HEADS
    w = small_ref[:, w_off:w_off + IDX_HEADS] * ((IDX_HEADS ** -0.5) * (IDX_DIM ** -0.5))
    qrow = j * LANE + lax.broadcasted_iota(jnp.int32, (LANE, kc), 0)

    def idx_body(c, _):
        off = pl.multiple_of(c * kc, kc)
        d = _dot(lhs, kit_ref[:, pl.ds(off, kc)])
        acc = jnp.zeros((LANE, kc), F32)
        for h in range(IDX_HEADS):
            acc = acc + w[:, h:h + 1] * jnp.maximum(d[h * LANE:(h + 1) * LANE], 0.0)
        krow = off + lax.broadcasted_iota(jnp.int32, (LANE, kc), 1)
        adm = (krow < N_META) | ((krow >= LANE) & ((krow // CHUNK) <= (qrow // CHUNK)))
        isc = jnp.where(adm, acc, -jnp.inf)
        bits = pltpu.bitcast(isc, jnp.int32)
        key_sc[:, pl.ds(off, kc)] = bits ^ ((bits >> 31) & 0x7FFFFFFF)
        return 0

    lax.fori_loop(0, nch, idx_body, 0)

    def search(b, t_u):
        bit = jnp.left_shift(jnp.int32(1), 31 - b)
        cand_u = t_u | bit
        cand_s = cand_u ^ INT_MIN

        def cnt_body(c, cnt):
            off = pl.multiple_of(c * kc, kc)
            ge = jnp.where(key_sc[:, pl.ds(off, kc)] >= cand_s, 1, 0)
            for t in range(bpc):
                cnt = cnt + ge[:, t * LANE:(t + 1) * LANE]
            return cnt

        cnt = lax.fori_loop(0, nch, cnt_body, jnp.zeros((LANE, LANE), jnp.int32))
        tot = jnp.sum(cnt, axis=-1, keepdims=True)
        return jnp.where(tot >= topk, cand_u, t_u)

    t_u = lax.fori_loop(0, 32, search, jnp.zeros((LANE, 1), jnp.int32))
    thr = jnp.maximum(t_u ^ INT_MIN, NEGINF_KEY + 1)

    near = pl.multiple_of(jnp.maximum(j - 1, 0) * LANE, LANE)
    nfar = jnp.maximum(j - 2 + bpc, 0) // bpc
    far_limit = (j - 1) * LANE

    for g in range(B_KV_HEADS):
        qg = jnp.concatenate(
            [qb_ref[:, (g * rep + r) * HEAD_DIM:(g * rep + r + 1) * HEAD_DIM] for r in range(rep)],
            axis=0)
        gsl = slice(g * HEAD_DIM, (g + 1) * HEAD_DIM)

        def attend(off, width, bias, limit, carry):
            m, l, acc = carry
            kk = kb_ref[pl.ds(off, width), gsl]
            vv = vb_ref[pl.ds(off, width), gsl]
            s = _dot_nt(qg, kk).reshape(rep, LANE, width)
            sel = key_sc[:, pl.ds(off, width)] >= thr
            if limit is not None:
                krow = off + lax.broadcasted_iota(jnp.int32, (LANE, width), 1)
                sel = sel & (krow < limit)
            if bias is not None:
                s = s + bias
            s = jnp.where(sel[None], s, NEG)
            m_new = jnp.maximum(m, jnp.max(s, axis=-1, keepdims=True))
            alpha = jnp.exp(m - m_new)
            p = jnp.exp(s - m_new)
            l = alpha * l + jnp.sum(p, axis=-1, keepdims=True)
            pv = _dot(p.reshape(rep * LANE, width).astype(BF16), vv)
            acc = alpha * acc + pv.reshape(rep, LANE, HEAD_DIM)
            return m_new, l, acc

        init = (jnp.full((rep, LANE, 1), -jnp.inf, F32), jnp.zeros((rep, LANE, 1), F32),
                jnp.zeros((rep, LANE, HEAD_DIM), F32))
        carry = attend(near, 2 * LANE, bias_ref[g * rep:(g + 1) * rep], None, init)

        def far_body(c, carry):
            return attend(pl.multiple_of(c * kc, kc), kc, None, far_limit, carry)

        _, l, acc = lax.fori_loop(0, nfar, far_body, carry)
        out = acc / l
        for r in range(rep):
            o_ref[:, (g * rep + r) * HEAD_DIM:(g * rep + r + 1) * HEAD_DIM] = out[r].astype(o_ref.dtype)


def _dsa(qi, small, kit, qk, kb, vb, bias, b_sz, t_len, n_heads, topk):
    nb = t_len // LANE
    kc = _pick(t_len, (384, 256))
    assert kc >= topk and t_len >= 2 * LANE
    hw = n_heads * HEAD_DIM
    kw = B_KV_HEADS * HEAD_DIM
    rowmap = lambda b, j: (b * nb + j, 0)
    return pl.pallas_call(
        functools.partial(_dsa_kernel, kc=kc, topk=topk, n_heads=n_heads),
        grid=(b_sz, nb),
        in_specs=[pl.BlockSpec((LANE, IDX_HEADS * IDX_DIM), rowmap),
                  pl.BlockSpec((LANE, LANE), rowmap),
                  pl.BlockSpec((None, 2 * IDX_DIM, t_len), lambda b, j: (b, 0, 0)),
                  pl.BlockSpec((LANE, hw), lambda b, j: (b * nb + j, 2)),
                  pl.BlockSpec((None, t_len, kw), lambda b, j: (b, 0, 0)),
                  pl.BlockSpec((None, t_len, kw), lambda b, j: (b, 0, 0)),
                  pl.BlockSpec((None, n_heads, LANE, 2 * LANE),
                               lambda b, j: (jnp.minimum(j, 2), 0, 0, 0))],
        out_specs=pl.BlockSpec((LANE, hw), rowmap),
        out_shape=jax.ShapeDtypeStruct((qi.shape[0], hw), BF16),
        scratch_shapes=[pltpu.VMEM((LANE, t_len), jnp.int32)],
        compiler_params=_params("parallel", "parallel"),
        name="dsa_attention",
    )(qi, small, kit, qk, kb, vb, bias)


def _norm_router_kernel(x_ref, g_ref, r_ref, o_ref, gate_ref, *, n_exp):
    x = x_ref[...]
    ms = jnp.mean(x * x, axis=-1, keepdims=True)
    y = x * lax.rsqrt(ms + EPS) * g_ref[...]
    o_ref[...] = y.astype(o_ref.dtype)
    logits = [jnp.sum(y * r_ref[e:e + 1, :], axis=-1, keepdims=True) for e in range(n_exp)]

    def argmax(vals):
        best, bi = vals[0], jnp.zeros_like(vals[0], dtype=jnp.int32)
        for e in range(1, n_exp):
            upd = vals[e] > best
            best = jnp.where(upd, vals[e], best)
            bi = jnp.where(upd, e, bi)
        return best, bi

    m1, i1 = argmax(logits)
    m2, i2 = argmax([jnp.where(i1 == e, -jnp.inf, logits[e]) for e in range(n_exp)])
    t = jnp.exp(m2 - m1)
    w1 = 1.0 / (1.0 + t)
    w2 = t / (1.0 + t)
    lane = lax.broadcasted_iota(jnp.int32, gate_ref.shape, 1)
    gate_ref[...] = jnp.where(lane == i1, w1, jnp.where(lane == i2, w2, 0.0))


def _norm_router(x, g, router):
    m, d = x.shape
    n_exp = router.shape[1]
    tm = _pick(m, (256, 128))
    return pl.pallas_call(
        functools.partial(_norm_router_kernel, n_exp=n_exp),
        grid=(m // tm,),
        in_specs=[pl.BlockSpec((tm, d), lambda i: (i, 0)),
                  pl.BlockSpec((1, d), lambda i: (0, 0)),
                  pl.BlockSpec((n_exp, d), lambda i: (0, 0))],
        out_specs=[pl.BlockSpec((tm, d), lambda i: (i, 0)),
                   pl.BlockSpec((tm, LANE), lambda i: (i, 0))],
        out_shape=[jax.ShapeDtypeStruct((m, d), BF16),
                   jax.ShapeDtypeStruct((m, LANE), F32)],
        compiler_params=_params("parallel"),
        name="norm_router",
    )(x, g.reshape(1, d), router.T.astype(F32))


def _dense_ffn(h2, h_res, w1, w3, w2):
    m, d = h2.shape
    f = w1.shape[1]
    fp = ((f + 1023) // 1024) * 1024 if f > 1024 else f
    pad = ((0, 0), (0, fp - f))
    w1p = jnp.pad(w1, pad).astype(BF16)
    w3p = jnp.pad(w3, pad).astype(BF16)
    w2p = jnp.pad(w2, ((0, fp - f), (0, 0))).astype(BF16)
    tm, tn, tk = _mm_tiles(m, fp, d)
    wspec = pl.BlockSpec((tk, tn), lambda i, j, k: (k, j))
    a = _mm([(h2, pl.BlockSpec((tm, tk), lambda i, j, k: (i, k)))],
            [(w1p, wspec), (w3p, wspec)], [0, 0], [], _ep_swiglu,
            jax.ShapeDtypeStruct((m, fp), BF16),
            pl.BlockSpec((tm, tn), lambda i, j, k: (i, j)),
            (m // tm, fp // tn, d // tk), tm, tn, "ffn_up")
    return _matmul(a, w2p, _ep_residual,
                   [(h_res, lambda tm, tn: (tm, tn), lambda i, j, k: (i, j))], F32, "ffn_down")


MOE_CAP = 512


def _moe_up_kernel(cnt_ref, h_ref, rr_ref, w1_ref, w3_ref, a_ref, xs_sc, *, cap, n_exp, cw):
    i, e, c, f = (pl.program_id(a) for a in range(4))
    active = (c == 0) | (c * cap < cnt_ref[i * n_exp + e])

    @pl.when(active & (f == 0))
    def _():
        rr = rr_ref[...]
        sel = jnp.where(rr - c * cap == lax.broadcasted_iota(jnp.int32, (cap, rr.shape[1]), 0),
                        1.0, 0.0).astype(BF16)
        for cc in range(h_ref.shape[1] // cw):
            xs_sc[:, cc * cw:(cc + 1) * cw] = _dot(sel, h_ref[:, cc * cw:(cc + 1) * cw]).astype(BF16)

    @pl.when(active)
    def _():
        xs = xs_sc[...]
        a = _dot(xs, w1_ref[...])
        b = _dot(xs, w3_ref[...])
        a_ref[...] = (a * _sigmoid(a) * b).astype(a_ref.dtype)

    @pl.when(jnp.logical_not(active))
    def _():
        a_ref[...] = jnp.zeros_like(a_ref)


def _moe_down_kernel(cnt_ref, a_ref, w2_ref, rk_ref, g_ref, res_ref, o_ref, acc, *, cap, n_exp, n_chunk):
    i, e, c = pl.program_id(0), pl.program_id(2), pl.program_id(3)
    active = (c == 0) | (c * cap < cnt_ref[i * n_exp + e])

    @pl.when((e == 0) & (c == 0))
    def _():
        acc[...] = res_ref[...]

    @pl.when(active)
    def _():
        y = _dot(a_ref[...], w2_ref[...]).astype(BF16)
        lane = lax.broadcasted_iota(jnp.int32, rk_ref.shape, 1)
        rank = jnp.sum(jnp.where(lane == e, rk_ref[...], 0.0), axis=-1, keepdims=True)
        gate = jnp.sum(jnp.where(lane == e, g_ref[...], 0.0), axis=-1, keepdims=True)
        tm = rk_ref.shape[0]
        scat = jnp.where(rank.astype(jnp.int32) - c * cap
                         == lax.broadcasted_iota(jnp.int32, (tm, cap), 1), 1.0, 0.0).astype(BF16)
        acc[...] += gate * _dot(scat, y)

    @pl.when((e == n_exp - 1) & (c == n_chunk - 1))
    def _():
        o_ref[...] = acc[...]


def _moe_ffn(h2, gates, h_res, w1, w3, w2):
    m, d = h2.shape
    n_exp, _, f = w1.shape
    tm = _pick(m, (1408, 768, 512, 256, 128))
    tn = _pick(f, (512, 256, 128))
    cap = min(MOE_CAP, tm)
    n_chunk = -(-tm // cap)
    n_i, n_f, n_n = m // tm, f // tn, d // tn

    sel = (gates[:, :n_exp] > 0).reshape(n_i, tm, n_exp)
    seli = sel.astype(jnp.int32)
    rank = jnp.where(sel, jnp.cumsum(seli, axis=1) - seli, -1).reshape(m, n_exp)
    cnt = jnp.sum(seli, axis=1).reshape(n_i * n_exp)
    rank_lanes = jnp.full((m, LANE), -1.0, F32).at[:, :n_exp].set(rank.astype(F32))
    rank_rows = rank.T.reshape(n_exp, 1, m)

    def act(i, e, c, cnt_ref):
        return (c == 0) | (c * cap < cnt_ref[i * n_exp + e])

    def wmap(i, e, c, j, cnt_ref):
        return (e, 0, jnp.where(act(i, e, c, cnt_ref), j, n_f - 1))

    a = pl.pallas_call(
        functools.partial(_moe_up_kernel, cap=cap, n_exp=n_exp, cw=_pick(d, (1024, 512, 256, 128))),
        grid_spec=pltpu.PrefetchScalarGridSpec(
            num_scalar_prefetch=1,
            grid=(n_i, n_exp, n_chunk, n_f),
            in_specs=[pl.BlockSpec((tm, d), lambda i, e, c, j, s: (i, 0)),
                      pl.BlockSpec((None, 1, tm), lambda i, e, c, j, s: (e, 0, i)),
                      pl.BlockSpec((None, d, tn), wmap),
                      pl.BlockSpec((None, d, tn), wmap)],
            out_specs=pl.BlockSpec((cap, tn), lambda i, e, c, j, s: ((i * n_exp + e) * n_chunk + c, j)),
            scratch_shapes=[pltpu.VMEM((cap, d), BF16)]),
        out_shape=jax.ShapeDtypeStruct((n_i * n_exp * n_chunk * cap, f), BF16),
        compiler_params=_params("parallel", "arbitrary", "arbitrary", "arbitrary"),
        name="moe_up",
    )(cnt, h2, rank_rows, w1.astype(BF16), w3.astype(BF16))

    def amap(i, n, e, c, cnt_ref):
        return ((i * n_exp + e) * n_chunk + jnp.where(act(i, e, c, cnt_ref), c, 0), 0)

    return pl.pallas_call(
        functools.partial(_moe_down_kernel, cap=cap, n_exp=n_exp, n_chunk=n_chunk),
        grid_spec=pltpu.PrefetchScalarGridSpec(
            num_scalar_prefetch=1,
            grid=(n_i, n_n, n_exp, n_chunk),
            in_specs=[pl.BlockSpec((cap, f), amap),
                      pl.BlockSpec((None, f, tn), lambda i, n, e, c, s: (e, 0, n)),
                      pl.BlockSpec((tm, LANE), lambda i, n, e, c, s: (i, 0)),
                      pl.BlockSpec((tm, LANE), lambda i, n, e, c, s: (i, 0)),
                      pl.BlockSpec((tm, tn), lambda i, n, e, c, s: (i, n))],
            out_specs=pl.BlockSpec((tm, tn), lambda i, n, e, c, s: (i, n)),
            scratch_shapes=[pltpu.VMEM((tm, tn), F32)]),
        out_shape=jax.ShapeDtypeStruct((m, d), F32),
        compiler_params=_params("parallel", "parallel", "arbitrary", "arbitrary"),
        name="moe_down",
    )(cnt, a, w2.astype(BF16), rank_lanes, gates, h_res)


def kernel(x, meta_tokens, rel_bias, norm_mix, w_in, b_forget, q_norm_a, k_norm_a, kv_norm_b, w_ukv_b, q_norm_b, k_norm_b, w_proj_a, w_proj_b, w_gate, b_gate, w_out, norm_ffn, dense_w1, dense_w3, dense_w2, router, moe_w1, moe_w3, moe_w2):
    b_sz, s_len, d = x.shape
    depth = norm_mix.shape[0]
    n_heads = d // (2 * HEAD_DIM)
    hw = n_heads * HEAD_DIM
    latent = kv_norm_b.shape[1]
    iqw = IDX_HEADS * IDX_DIM
    assert s_len % LANE == 0 and N_META <= LANE and n_heads % B_KV_HEADS == 0
    t_len = LANE + s_len
    m = b_sz * t_len
    topk = min(TOPK_MAX, s_len // 4)
    scale = HEAD_DIM ** -0.5

    sizes = (hw, hw, hw, n_heads, hw, latent, iqw, IDX_DIM, IDX_HEADS)
    offs = np.concatenate([[0], np.cumsum(sizes)])
    o_qa, o_ka, o_va, o_fa, o_qb, o_cb, o_qi, o_ki, o_wi = (int(v) for v in offs[:-1])

    meta = jnp.broadcast_to(meta_tokens[None].astype(F32), (b_sz, N_META, d))
    h_res = jnp.concatenate(
        [meta, jnp.zeros((b_sz, LANE - N_META, d), F32), x.astype(F32)], axis=1).reshape(m, d)

    bias = _bias_strips(rel_bias)

    for l in range(depth):
        w = w_in[l]
        w_qkq = jnp.concatenate([w[:, o_qa:o_qa + hw], w[:, o_ka:o_ka + hw], w[:, o_qb:o_qb + hw]],
                                axis=1).astype(BF16)
        w_va = w[:, o_va:o_va + hw].astype(BF16)
        w_qi = w[:, o_qi:o_qi + iqw].astype(BF16)
        w_cb = w[:, o_cb:o_cb + latent].astype(BF16)
        n_small = IDX_DIM + n_heads + IDX_HEADS
        assert n_small <= LANE and IDX_DIM + n_heads <= IDX_DIM + IDX_HEADS
        w_small = jnp.concatenate(
            [w[:, o_ki:o_ki + IDX_DIM], w[:, o_fa:o_fa + n_heads],
             jnp.zeros((d, IDX_HEADS - n_heads), F32), w[:, o_wi:o_wi + IDX_HEADS],
             jnp.zeros((d, LANE - IDX_DIM - 2 * IDX_HEADS), F32)], axis=1).astype(BF16)
        gains = jnp.concatenate([jnp.tile(q_norm_a[l] * scale, n_heads), jnp.tile(k_norm_a[l], n_heads),
                                 jnp.tile(q_norm_b[l] * scale, n_heads)]).reshape(1, 3 * hw).astype(F32)
        brow = jnp.zeros((1, LANE), F32).at[0, IDX_DIM:IDX_DIM + n_heads].set(b_forget[l].astype(F32))

        h = _rmsnorm(h_res, norm_mix[l])
        qkq = _matmul(h, w_qkq, _ep_headnorm,
                      [(gains, lambda tm, tn: (1, tn), lambda i, j, k: (0, j))], BF16, "proj_qkq")
        va = _matmul(h, w_va, _ep_plain, [], BF16, "proj_va")
        qi = _matmul(h, w_qi, _ep_plain, [], BF16, "proj_qi")
        cb = _matmul(h, w_cb, _ep_plain, [], BF16, "proj_cb")
        small = _matmul(h, w_small, _ep_plain, [], F32, "proj_small")
        gate = _matmul(h, w_gate[l].astype(BF16), _ep_gate,
                       [(b_gate[l].reshape(1, 2 * d).astype(F32), lambda tm, tn: (1, tn),
                         lambda i, j, k: (0, j))], BF16, "proj_gate")

        c, kit, kb, vb = _post(small, cb, brow, kv_norm_b[l].reshape(1, latent).astype(F32),
                               w_ukv_b[l].astype(BF16), k_norm_b[l].reshape(1, HEAD_DIM).astype(F32),
                               b_sz, t_len, n_heads)

        oa = _fox(qkq.reshape(b_sz, t_len, 3 * hw), va.reshape(b_sz, t_len, hw),
                  c.reshape(b_sz, n_heads, 1, t_len), b_sz, t_len, n_heads).reshape(m, hw)
        kw = B_KV_HEADS * HEAD_DIM
        ob = _dsa(qi, small, kit, qkq, kb.reshape(b_sz, t_len, kw), vb.reshape(b_sz, t_len, kw),
                  bias, b_sz, t_len, n_heads, topk)

        tm, tn, tk = _mm_tiles(m, d, hw)
        nj = d // tn
        xspec = pl.BlockSpec((tm, tk), lambda i, j, k: (i, k))
        wspec = pl.BlockSpec((tk, tn), lambda i, j, k: (k, j))
        merged = _mm([(oa, xspec), (ob, xspec)],
                     [(w_proj_a[l].astype(BF16), wspec), (w_proj_b[l].astype(BF16), wspec)], [0, 1],
                     [(gate, pl.BlockSpec((tm, tn), lambda i, j, k: (i, j))),
                      (gate, pl.BlockSpec((tm, tn), lambda i, j, k: (i, j + nj)))],
                     _ep_merge, jax.ShapeDtypeStruct((m, d), BF16),
                     pl.BlockSpec((tm, tn), lambda i, j, k: (i, j)),
                     (m // tm, nj, hw // tk), tm, tn, "merge_proj")
        h_res = _matmul(merged, w_out[l].astype(BF16), _ep_residual,
                        [(h_res, lambda tm, tn: (tm, tn), lambda i, j, k: (i, j))], F32, "out_proj")

        jj = l // 2
        if l % 2 == 0:
            h2 = _rmsnorm(h_res, norm_ffn[l])
            h_res = _dense_ffn(h2, h_res, dense_w1[jj], dense_w3[jj], dense_w2[jj])
        else:
            h2, gates = _norm_router(h_res, norm_ffn[l], router[jj])
            h_res = _moe_ffn(h2, gates, h_res, moe_w1[jj], moe_w3[jj], moe_w2[jj])

    return h_res.reshape(b_sz, t_len, d)[:, LANE:, :]
```

```python
import functools

import numpy as np
import jax
import jax.numpy as jnp
from jax import lax
from jax.experimental import pallas as pl
from jax.experimental.pallas import tpu as pltpu

F32 = jnp.float32
BF16 = jnp.bfloat16

CHUNK = 64
N_META = 16
HEAD_DIM = 128
B_KV_HEADS = 2
IDX_HEADS = 16
IDX_DIM = 64
TOPK_MAX = 256
REL_BUCKETS = 32
REL_MAX_DIST = 128
EPS = 1e-6
NEG = -1e30

LANE = 128
VMEM_LIMIT_BYTES = 56 * 1024 * 1024
INT_MIN = -2147483648
NEGINF_KEY = -2139095041


def _pick(n, prefs):
    for p in prefs:
        if n % p == 0:
            return p
    return n


def _params(*sem):
    return pltpu.CompilerParams(dimension_semantics=sem, vmem_limit_bytes=VMEM_LIMIT_BYTES)


def _split3(x):
    hi = x.astype(BF16)
    r1 = x - hi.astype(F32)
    mid = r1.astype(BF16)
    lo = (r1 - mid.astype(F32)).astype(BF16)
    return hi, mid, lo


def _dot(a, b):
    return jnp.dot(a, b, preferred_element_type=F32)


def _dot_nt(a, b):
    return lax.dot_general(a, b, (((1,), (1,)), ((), ())), preferred_element_type=F32)


def _sigmoid(x):
    return 1.0 / (1.0 + jnp.exp(-x))


def _rmsnorm_kernel(x_ref, g_ref, o_ref):
    x = x_ref[...]
    ms = jnp.mean(x * x, axis=-1, keepdims=True)
    o_ref[...] = (x * lax.rsqrt(ms + EPS) * g_ref[...]).astype(o_ref.dtype)


def _rmsnorm(x, g):
    m, d = x.shape
    tm = _pick(m, (256, 128))
    return pl.pallas_call(
        _rmsnorm_kernel,
        grid=(m // tm,),
        in_specs=[pl.BlockSpec((tm, d), lambda i: (i, 0)),
                  pl.BlockSpec((1, d), lambda i: (0, 0))],
        out_specs=pl.BlockSpec((tm, d), lambda i: (i, 0)),
        out_shape=jax.ShapeDtypeStruct((m, d), BF16),
        compiler_params=_params("parallel"),
        name="rmsnorm",
    )(x, g.reshape(1, d))


def _mm_kernel(*refs, n_x, n_w, pair_x, n_extra, nk, epilogue):
    xs = refs[:n_x]
    ws = refs[n_x:n_x + n_w]
    extras = refs[n_x + n_w:n_x + n_w + n_extra]
    o_ref = refs[n_x + n_w + n_extra]
    accs = refs[n_x + n_w + n_extra + 1:]

    def finish(vals):
        o_ref[...] = epilogue(vals, [e[...] for e in extras]).astype(o_ref.dtype)

    if nk == 1:
        finish([_dot(xs[pair_x[p]][...], ws[p][...]) for p in range(n_w)])
        return

    k = pl.program_id(2)
    for p in range(n_w):
        d = _dot(xs[pair_x[p]][...], ws[p][...])

        @pl.when(k == 0)
        def _(d=d, p=p):
            accs[p][...] = d

        @pl.when(k > 0)
        def _(d=d, p=p):
            accs[p][...] += d

    @pl.when(k == nk - 1)
    def _():
        finish([a[...] for a in accs])


def _mm(xs, ws, pair_x, extras, epilogue, out_shape, out_spec, grid, tm, tn, name):
    nk = grid[2]
    arrays = [a for a, _ in xs] + [a for a, _ in ws] + [a for a, _ in extras]
    specs = [s for _, s in xs] + [s for _, s in ws] + [s for _, s in extras]
    scratch = [] if nk == 1 else [pltpu.VMEM((tm, tn), F32) for _ in ws]
    return pl.pallas_call(
        functools.partial(_mm_kernel, n_x=len(xs), n_w=len(ws), pair_x=tuple(pair_x),
                          n_extra=len(extras), nk=nk, epilogue=epilogue),
        grid=grid,
        in_specs=specs,
        out_specs=out_spec,
        out_shape=out_shape,
        scratch_shapes=scratch,
        compiler_params=_params("parallel", "parallel", "arbitrary"),
        name=name,
    )(*arrays)


def _mm_tiles(m, n, k):
    tm = _pick(m, (1408, 1024, 768, 512, 384, 256, 128))
    tn = _pick(n, (1024, 512, 256, 128))
    tk = _pick(k, (1024, 512, 256, 128))
    return tm, tn, tk


def _matmul(x, w, epilogue, extras, out_dtype, name):
    m, kd = x.shape
    n = w.shape[1]
    tm, tn, tk = _mm_tiles(m, n, kd)
    ex = [(a, pl.BlockSpec(bs(tm, tn), im)) for a, bs, im in extras]
    return _mm(
        [(x, pl.BlockSpec((tm, tk), lambda i, j, k: (i, k)))],
        [(w, pl.BlockSpec((tk, tn), lambda i, j, k: (k, j)))],
        [0], ex, epilogue,
        jax.ShapeDtypeStruct((m, n), out_dtype),
        pl.BlockSpec((tm, tn), lambda i, j, k: (i, j)),
        (m // tm, n // tn, kd // tk), tm, tn, name)


def _ep_plain(vals, extras):
    return vals[0]


def _ep_headnorm(vals, extras):
    a, g = vals[0], extras[0]
    outs = []
    for c in range(a.shape[1] // LANE):
        blk = a[:, c * LANE:(c + 1) * LANE]
        ms = jnp.mean(blk * blk, axis=-1, keepdims=True)
        outs.append(blk * lax.rsqrt(ms + EPS) * g[:, c * LANE:(c + 1) * LANE])
    return jnp.concatenate(outs, axis=1)


def _ep_gate(vals, extras):
    return _sigmoid(vals[0] + extras[0])


def _ep_merge(vals, extras):
    return extras[0].astype(F32) * vals[0] + extras[1].astype(F32) * vals[1]


def _ep_residual(vals, extras):
    return extras[0] + vals[0]


def _ep_swiglu(vals, extras):
    a, b = vals
    return a * _sigmoid(a) * b


def _post_kernel(small_ref, cb_ref, brow_ref, kvg_ref, wukv_ref, kng_ref,
                 c_ref, kit_ref, kb_ref, vb_ref, carry_ref, *, tq, n_kv):
    i = pl.program_id(1)
    sm = small_ref[...]
    row = i * tq + lax.broadcasted_iota(jnp.int32, (tq, 1), 0)
    valid = (row < N_META) | (row >= LANE)

    z = sm + brow_ref[...]
    lf = jnp.minimum(z, 0.0) - jnp.log1p(jnp.exp(-jnp.abs(z)))
    lf = jnp.where(valid, lf, 0.0)
    r_i = lax.broadcasted_iota(jnp.int32, (tq, tq), 0)
    c_i = lax.broadcasted_iota(jnp.int32, (tq, tq), 1)
    tri = jnp.where(r_i >= c_i, 1.0, 0.0).astype(BF16)
    hi, mid, lo = _split3(lf)
    cs = _dot(tri, hi) + _dot(tri, mid) + _dot(tri, lo)

    @pl.when(i == 0)
    def _():
        carry_ref[...] = jnp.zeros_like(carry_ref)

    cs = cs + carry_ref[...]
    carry_ref[...] = cs[tq - 1:tq, :]
    cs = jnp.where(valid, cs, -NEG)
    c_ref[...] = cs.T[IDX_DIM:IDX_DIM + c_ref.shape[0], :]

    kit = sm.T[0:IDX_DIM, :].astype(BF16)
    kit_ref[0:IDX_DIM, :] = kit
    kit_ref[IDX_DIM:2 * IDX_DIM, :] = kit

    cbv = cb_ref[...].astype(F32)
    ms = jnp.mean(cbv * cbv, axis=-1, keepdims=True)
    kvn = (cbv * lax.rsqrt(ms + EPS) * kvg_ref[...]).astype(BF16)
    kv = _dot(kvn, wukv_ref[...])
    kw = n_kv * HEAD_DIM
    for g in range(n_kv):
        blk = kv[:, g * HEAD_DIM:(g + 1) * HEAD_DIM]
        msk = jnp.mean(blk * blk, axis=-1, keepdims=True)
        kb_ref[:, g * HEAD_DIM:(g + 1) * HEAD_DIM] = (
            blk * lax.rsqrt(msk + EPS) * kng_ref[...]).astype(BF16)
    vb_ref[...] = kv[:, kw:2 * kw].astype(BF16)


def _post(small, cb, brow, kvg, wukv, kng, b_sz, t_len, n_heads):
    m = small.shape[0]
    latent = cb.shape[1]
    tq = _pick(t_len, (384, 256, 128))
    nq = t_len // tq
    kw = B_KV_HEADS * HEAD_DIM
    rowmap = lambda b, i: (b * nq + i, 0)
    const = lambda b, i: (0, 0)
    return pl.pallas_call(
        functools.partial(_post_kernel, tq=tq, n_kv=B_KV_HEADS),
        grid=(b_sz, nq),
        in_specs=[pl.BlockSpec((tq, LANE), rowmap),
                  pl.BlockSpec((tq, latent), rowmap),
                  pl.BlockSpec((1, LANE), const),
                  pl.BlockSpec((1, latent), const),
                  pl.BlockSpec((latent, 2 * kw), const),
                  pl.BlockSpec((1, HEAD_DIM), const)],
        out_specs=[pl.BlockSpec((None, n_heads, tq), lambda b, i: (b, 0, i)),
                   pl.BlockSpec((None, 2 * IDX_DIM, tq), lambda b, i: (b, 0, i)),
                   pl.BlockSpec((tq, kw), rowmap),
                   pl.BlockSpec((tq, kw), rowmap)],
        out_shape=[jax.ShapeDtypeStruct((b_sz, n_heads, t_len), F32),
                   jax.ShapeDtypeStruct((b_sz, 2 * IDX_DIM, t_len), BF16),
                   jax.ShapeDtypeStruct((m, kw), BF16),
                   jax.ShapeDtypeStruct((m, kw), BF16)],
        scratch_shapes=[pltpu.VMEM((1, LANE), F32)],
        compiler_params=_params("parallel", "arbitrary"),
        name="post_proj",
    )(small, cb, brow, kvg, wukv, kng)


def _fox_kernel(q_ref, k_ref, v_ref, c_ref, o_ref, *, tq, hpb):
    i = pl.program_id(2)

    def step(j, carry, diagonal):
        off = pl.multiple_of(j * tq, tq)
        out = []
        for hh in range(hpb):
            m, l, acc = carry[hh]
            hs = slice(hh * HEAD_DIM, (hh + 1) * HEAD_DIM)
            s = _dot_nt(q_ref[:, hs], k_ref[pl.ds(off, tq), hs]) - c_ref[hh, :, pl.ds(off, tq)]
            if diagonal:
                r_i = lax.broadcasted_iota(jnp.int32, (tq, tq), 0)
                c_i = lax.broadcasted_iota(jnp.int32, (tq, tq), 1)
                s = jnp.where(c_i <= r_i, s, NEG)
            m_new = jnp.maximum(m, jnp.max(s, axis=-1, keepdims=True))
            alpha = jnp.exp(m - m_new)
            p = jnp.exp(s - m_new)
            l = alpha * l + jnp.sum(p, axis=-1, keepdims=True)
            acc = alpha * acc + _dot(p.astype(BF16), v_ref[pl.ds(off, tq), hs])
            out.append((m_new, l, acc))
        return tuple(out)

    init = tuple((jnp.full((tq, 1), -jnp.inf, F32), jnp.zeros((tq, 1), F32),
                  jnp.zeros((tq, HEAD_DIM), F32)) for _ in range(hpb))
    carry = lax.fori_loop(0, i, lambda j, c: step(j, c, False), init)
    carry = step(i, carry, True)
    for hh in range(hpb):
        _, l, acc = carry[hh]
        o_ref[:, hh * HEAD_DIM:(hh + 1) * HEAD_DIM] = (acc / l).astype(o_ref.dtype)


def _fox(qk, v, c, b_sz, t_len, n_heads):
    tq = _pick(t_len, (384, 256, 128))
    nq = t_len // tq
    hpb = _pick(n_heads, (4, 2, 1))
    nhb = n_heads // hpb
    hbw = hpb * HEAD_DIM
    return pl.pallas_call(
        functools.partial(_fox_kernel, tq=tq, hpb=hpb),
        grid=(b_sz, nhb, nq),
        in_specs=[pl.BlockSpec((None, tq, hbw), lambda b, h, i: (b, i, h)),
                  pl.BlockSpec((None, t_len, hbw), lambda b, h, i: (b, 0, nhb + h)),
                  pl.BlockSpec((None, t_len, hbw), lambda b, h, i: (b, 0, h)),
                  pl.BlockSpec((None, hpb, 1, t_len), lambda b, h, i: (b, h, 0, 0))],
        out_specs=pl.BlockSpec((None, tq, hbw), lambda b, h, i: (b, i, h)),
        out_shape=jax.ShapeDtypeStruct((b_sz, t_len, n_heads * HEAD_DIM), BF16),
        compiler_params=_params("parallel", "parallel", "parallel"),
        name="fox_attention",
    )(qk, qk, v, c)


def _rel_bucket_int(rel):
    nb = REL_BUCKETS // 2
    max_exact = nb // 2
    ret = nb if rel > 0 else 0
    n = abs(rel)
    if n < max_exact:
        return ret + n
    span = nb - max_exact
    k = 0
    while (n ** span) * (max_exact ** (k + 1)) >= (REL_MAX_DIST ** (k + 1)) * (max_exact ** span):
        k += 1
        if max_exact + k >= nb - 1:
            break
    return ret + min(max_exact + k, nb - 1)


def _bucket_strips():
    far = _rel_bucket_int(-(LANE + 1))
    assert far == _rel_bucket_int(-(10 ** 6))
    q = np.arange(LANE)[:, None]
    k = np.arange(LANE)[None, :]
    lut = {r: _rel_bucket_int(r) for r in range(-3 * LANE, 3 * LANE)}
    f = np.vectorize(lambda r: lut[int(r)])
    own = f(k - q)
    prev = f(k - q - LANE)
    meta1 = f(k - (q + N_META))
    farblk = np.full((LANE, LANE), far)
    strips = np.stack([np.concatenate([own, farblk], 1),
                       np.concatenate([meta1, own], 1),
                       np.concatenate([prev, own], 1)])
    return strips.reshape(3, 1, LANE * 2 * LANE).astype(np.int32), far


def _bias_kernel(b_ref, r_ref, o_ref, *, far):
    bk = b_ref[...]
    w = bk.shape[1]
    oh = jnp.where(lax.broadcasted_iota(jnp.int32, (LANE, w), 0) == bk, 1.0, 0.0).astype(BF16)
    rb = r_ref[...]
    rb = rb - rb[:, far:far + 1]
    hi, mid, lo = _split3(rb)
    o_ref[...] = _dot(hi, oh) + _dot(mid, oh) + _dot(lo, oh)


def _bias_strips(rel_bias):
    n_heads = rel_bias.shape[1]
    hp = max(16, n_heads)
    strips, far = _bucket_strips()
    w_all = strips.shape[2]
    wc = 4096
    rbt = jnp.zeros((hp, LANE), F32).at[:n_heads, :REL_BUCKETS].set(rel_bias.T.astype(F32))
    out = pl.pallas_call(
        functools.partial(_bias_kernel, far=far),
        grid=(3, w_all // wc),
        in_specs=[pl.BlockSpec((None, 1, wc), lambda s, c: (s, 0, c)),
                  pl.BlockSpec((hp, LANE), lambda s, c: (0, 0))],
        out_specs=pl.BlockSpec((None, hp, wc), lambda s, c: (s, 0, c)),
        out_shape=jax.ShapeDtypeStruct((3, hp, w_all), F32),
        compiler_params=_params("parallel", "parallel"),
        name="rel_bias_strips",
    )(jnp.asarray(strips), rbt)
    return out[:, :n_heads].reshape(3, n_heads, LANE, 2 * LANE)


def _dsa_kernel(qi_ref, small_ref, kit_ref, qb_ref, kb_ref, vb_ref, bias_ref, o_ref,
                key_sc, *, kc, topk, n_heads):
    j = pl.program_id(1)
    bpc = kc // LANE
    nch = j // bpc + 1
    rep = n_heads // B_KV_HEADS

    lane = lax.broadcasted_iota(jnp.int32, (LANE, LANE), 1)
    lo_m = jnp.where(lane < IDX_DIM, 1.0, 0.0).astype(BF16)
    hi_m = jnp.where(lane >= IDX_DIM, 1.0, 0.0).astype(BF16)
    parts = []
    for p in range(IDX_HEADS // 2):
        blk = qi_ref[:, p * LANE:(p + 1) * LANE]
        parts.append(blk * lo_m)
        parts.append(blk * hi_m)
    lhs = jnp.concatenate(parts, axis=0)
    w_off = IDX_DIM + IDX_HEADS
    w = small_ref[:, w_off:w_off + IDX_HEADS] * ((IDX_HEADS ** -0.5) * (IDX_DIM ** -0.5))
    qrow = j * LANE + lax.broadcasted_iota(jnp.int32, (LANE, kc), 0)

    def idx_body(c, _):
        off = pl.multiple_of(c * kc, kc)
        d = _dot(lhs, kit_ref[:, pl.ds(off, kc)])
        acc = jnp.zeros((LANE, kc), F32)
        for h in range(IDX_HEADS):
            acc = acc + w[:, h:h + 1] * jnp.maximum(d[h * LANE:(h + 1) * LANE], 0.0)
        krow = off + lax.broadcasted_iota(jnp.int32, (LANE, kc), 1)
        adm = (krow < N_META) | ((krow >= LANE) & ((krow // CHUNK) <= (qrow // CHUNK)))
        isc = jnp.where(adm, acc, -jnp.inf)
        bits = pltpu.bitcast(isc, jnp.int32)
        key_sc[:, pl.ds(off, kc)] = bits ^ ((bits >> 31) & 0x7FFFFFFF)
        return 0

    lax.fori_loop(0, nch, idx_body, 0)

    def search(b, t_u):
        bit = jnp.left_shift(jnp.int32(1), 31 - b)
        cand_u = t_u | bit
        cand_s = cand_u ^ INT_MIN

        def cnt_body(c, cnt):
            off = pl.multiple_of(c * kc, kc)
            ge = jnp.where(key_sc[:, pl.ds(off, kc)] >= cand_s, 1, 0)
            for t in range(bpc):
                cnt = cnt + ge[:, t * LANE:(t + 1) * LANE]
            return cnt

        cnt = lax.fori_loop(0, nch, cnt_body, jnp.zeros((LANE, LANE), jnp.int32))
        tot = jnp.sum(cnt, axis=-1, keepdims=True)
        return jnp.where(tot >= topk, cand_u, t_u)

    t_u = lax.fori_loop(0, 32, search, jnp.zeros((LANE, 1), jnp.int32))
    thr = jnp.maximum(t_u ^ INT_MIN, NEGINF_KEY + 1)

    near = pl.multiple_of(jnp.maximum(j - 1, 0) * LANE, LANE)
    nfar = jnp.maximum(j - 2 + bpc, 0) // bpc
    far_limit = (j - 1) * LANE

    qgs = [jnp.concatenate(
        [qb_ref[:, (g * rep + r) * HEAD_DIM:(g * rep + r + 1) * HEAD_DIM] for r in range(rep)],
        axis=0) for g in range(B_KV_HEADS)]

    def attend(off, width, with_bias, limit, carries):
        sel = key_sc[:, pl.ds(off, width)] >= thr
        if limit is not None:
            krow = off + lax.broadcasted_iota(jnp.int32, (LANE, width), 1)
            sel = sel & (krow < limit)
        out = []
        for g in range(B_KV_HEADS):
            m, l, acc = carries[g]
            gsl = slice(g * HEAD_DIM, (g + 1) * HEAD_DIM)
            s = _dot_nt(qgs[g], kb_ref[pl.ds(off, width), gsl]).reshape(rep, LANE, width)
            if with_bias:
                s = s + bias_ref[g * rep:(g + 1) * rep]
            s = jnp.where(sel[None], s, NEG)
            m_new = jnp.maximum(m, jnp.max(s, axis=-1, keepdims=True))
            alpha = jnp.exp(m - m_new)
            p = jnp.exp(s - m_new)
            l = alpha * l + jnp.sum(p, axis=-1, keepdims=True)
            pv = _dot(p.reshape(rep * LANE, width).astype(BF16), vb_ref[pl.ds(off, width), gsl])
            acc = alpha * acc + pv.reshape(rep, LANE, HEAD_DIM)
            out.append((m_new, l, acc))
        return tuple(out)

    init = tuple((jnp.full((rep, LANE, 1), -jnp.inf, F32), jnp.zeros((rep, LANE, 1), F32),
                  jnp.zeros((rep, LANE, HEAD_DIM), F32)) for _ in range(B_KV_HEADS))
    carries = attend(near, 2 * LANE, True, None, init)

    def far_body(c, carries):
        return attend(pl.multiple_of(c * kc, kc), kc, False, far_limit, carries)

    carries = lax.fori_loop(0, nfar, far_body, carries)
    for g in range(B_KV_HEADS):
        _, l, acc = carries[g]
        out = acc / l
        for r in range(rep):
            o_ref[:, (g * rep + r) * HEAD_DIM:(g * rep + r + 1) * HEAD_DIM] = out[r].astype(o_ref.dtype)


def _dsa(qi, small, kit, qk, kb, vb, bias, b_sz, t_len, n_heads, topk):
    nb = t_len // LANE
    kc = _pick(t_len, (384, 256))
    assert kc >= topk and t_len >= 2 * LANE
    hw = n_heads * HEAD_DIM
    kw = B_KV_HEADS * HEAD_DIM
    rowmap = lambda b, j: (b * nb + j, 0)
    return pl.pallas_call(
        functools.partial(_dsa_kernel, kc=kc, topk=topk, n_heads=n_heads),
        grid=(b_sz, nb),
        in_specs=[pl.BlockSpec((LANE, IDX_HEADS * IDX_DIM), rowmap),
                  pl.BlockSpec((LANE, LANE), rowmap),
                  pl.BlockSpec((None, 2 * IDX_DIM, t_len), lambda b, j: (b, 0, 0)),
                  pl.BlockSpec((LANE, hw), lambda b, j: (b * nb + j, 2)),
                  pl.BlockSpec((None, t_len, kw), lambda b, j: (b, 0, 0)),
                  pl.BlockSpec((None, t_len, kw), lambda b, j: (b, 0, 0)),
                  pl.BlockSpec((None, n_heads, LANE, 2 * LANE),
                               lambda b, j: (jnp.minimum(j, 2), 0, 0, 0))],
        out_specs=pl.BlockSpec((LANE, hw), rowmap),
        out_shape=jax.ShapeDtypeStruct((qi.shape[0], hw), BF16),
        scratch_shapes=[pltpu.VMEM((LANE, t_len), jnp.int32)],
        compiler_params=_params("parallel", "parallel"),
        name="dsa_attention",
    )(qi, small, kit, qk, kb, vb, bias)


def _norm_router_kernel(x_ref, g_ref, r_ref, o_ref, gate_ref, *, n_exp):
    x = x_ref[...]
    ms = jnp.mean(x * x, axis=-1, keepdims=True)
    y = x * lax.rsqrt(ms + EPS) * g_ref[...]
    o_ref[...] = y.astype(o_ref.dtype)
    logits = [jnp.sum(y * r_ref[e:e + 1, :], axis=-1, keepdims=True) for e in range(n_exp)]

    def argmax(vals):
        best, bi = vals[0], jnp.zeros_like(vals[0], dtype=jnp.int32)
        for e in range(1, n_exp):
            upd = vals[e] > best
            best = jnp.where(upd, vals[e], best)
            bi = jnp.where(upd, e, bi)
        return best, bi

    m1, i1 = argmax(logits)
    m2, i2 = argmax([jnp.where(i1 == e, -jnp.inf, logits[e]) for e in range(n_exp)])
    t = jnp.exp(m2 - m1)
    w1 = 1.0 / (1.0 + t)
    w2 = t / (1.0 + t)
    lane = lax.broadcasted_iota(jnp.int32, gate_ref.shape, 1)
    gate_ref[...] = jnp.where(lane == i1, w1, jnp.where(lane == i2, w2, 0.0))


def _norm_router(x, g, router):
    m, d = x.shape
    n_exp = router.shape[1]
    tm = _pick(m, (256, 128))
    return pl.pallas_call(
        functools.partial(_norm_router_kernel, n_exp=n_exp),
        grid=(m // tm,),
        in_specs=[pl.BlockSpec((tm, d), lambda i: (i, 0)),
                  pl.BlockSpec((1, d), lambda i: (0, 0)),
                  pl.BlockSpec((n_exp, d), lambda i: (0, 0))],
        out_specs=[pl.BlockSpec((tm, d), lambda i: (i, 0)),
                   pl.BlockSpec((tm, LANE), lambda i: (i, 0))],
        out_shape=[jax.ShapeDtypeStruct((m, d), BF16),
                   jax.ShapeDtypeStruct((m, LANE), F32)],
        compiler_params=_params("parallel"),
        name="norm_router",
    )(x, g.reshape(1, d), router.T.astype(F32))


def _dense_ffn(h2, h_res, w1, w3, w2):
    m, d = h2.shape
    f = w1.shape[1]
    fp = ((f + 1023) // 1024) * 1024 if f > 1024 else f
    pad = ((0, 0), (0, fp - f))
    w1p = jnp.pad(w1, pad).astype(BF16)
    w3p = jnp.pad(w3, pad).astype(BF16)
    w2p = jnp.pad(w2, ((0, fp - f), (0, 0))).astype(BF16)
    tm, tn, tk = _mm_tiles(m, fp, d)
    wspec = pl.BlockSpec((tk, tn), lambda i, j, k: (k, j))
    a = _mm([(h2, pl.BlockSpec((tm, tk), lambda i, j, k: (i, k)))],
            [(w1p, wspec), (w3p, wspec)], [0, 0], [], _ep_swiglu,
            jax.ShapeDtypeStruct((m, fp), BF16),
            pl.BlockSpec((tm, tn), lambda i, j, k: (i, j)),
            (m // tm, fp // tn, d // tk), tm, tn, "ffn_up")
    return _matmul(a, w2p, _ep_residual,
                   [(h_res, lambda tm, tn: (tm, tn), lambda i, j, k: (i, j))], F32, "ffn_down")


MOE_CAP = 512


def _moe_up_kernel(cnt_ref, h_ref, rr_ref, w1_ref, w3_ref, a_ref, xs_sc, *, cap, n_exp, cw):
    i, e, c, f = (pl.program_id(a) for a in range(4))
    active = (c == 0) | (c * cap < cnt_ref[i * n_exp + e])

    @pl.when(active & (f == 0))
    def _():
        rr = rr_ref[...]
        sel = jnp.where(rr - c * cap == lax.broadcasted_iota(jnp.int32, (cap, rr.shape[1]), 0),
                        1.0, 0.0).astype(BF16)
        for cc in range(h_ref.shape[1] // cw):
            xs_sc[:, cc * cw:(cc + 1) * cw] = _dot(sel, h_ref[:, cc * cw:(cc + 1) * cw]).astype(BF16)

    @pl.when(active)
    def _():
        xs = xs_sc[...]
        a = _dot(xs, w1_ref[...])
        b = _dot(xs, w3_ref[...])
        a_ref[...] = (a * _sigmoid(a) * b).astype(a_ref.dtype)

    @pl.when(jnp.logical_not(active))
    def _():
        a_ref[...] = jnp.zeros_like(a_ref)


def _moe_down_kernel(cnt_ref, a_ref, w2_ref, rk_ref, g_ref, res_ref, o_ref, acc, *, cap, n_exp, n_chunk):
    i, e, c = pl.program_id(0), pl.program_id(2), pl.program_id(3)
    active = (c == 0) | (c * cap < cnt_ref[i * n_exp + e])

    @pl.when((e == 0) & (c == 0))
    def _():
        acc[...] = res_ref[...]

    @pl.when(active)
    def _():
        y = _dot(a_ref[...], w2_ref[...]).astype(BF16)
        lane = lax.broadcasted_iota(jnp.int32, rk_ref.shape, 1)
        rank = jnp.sum(jnp.where(lane == e, rk_ref[...], 0.0), axis=-1, keepdims=True)
        gate = jnp.sum(jnp.where(lane == e, g_ref[...], 0.0), axis=-1, keepdims=True)
        tm = rk_ref.shape[0]
        scat = jnp.where(rank.astype(jnp.int32) - c * cap
                         == lax.broadcasted_iota(jnp.int32, (tm, cap), 1), 1.0, 0.0).astype(BF16)
        acc[...] += gate * _dot(scat, y)

    @pl.when((e == n_exp - 1) & (c == n_chunk - 1))
    def _():
        o_ref[...] = acc[...]


def _moe_ffn(h2, gates, h_res, w1, w3, w2):
    m, d = h2.shape
    n_exp, _, f = w1.shape
    tm = _pick(m, (1408, 768, 512, 256, 128))
    tn = _pick(f, (512, 256, 128))
    cap = min(MOE_CAP, tm)
    n_chunk = -(-tm // cap)
    n_i, n_f, n_n = m // tm, f // tn, d // tn

    sel = (gates[:, :n_exp] > 0).reshape(n_i, tm, n_exp)
    seli = sel.astype(jnp.int32)
    rank = jnp.where(sel, jnp.cumsum(seli, axis=1) - seli, -1).reshape(m, n_exp)
    cnt = jnp.sum(seli, axis=1).reshape(n_i * n_exp)
    rank_lanes = jnp.full((m, LANE), -1.0, F32).at[:, :n_exp].set(rank.astype(F32))
    rank_rows = rank.T.reshape(n_exp, 1, m)

    def act(i, e, c, cnt_ref):
        return (c == 0) | (c * cap < cnt_ref[i * n_exp + e])

    def wmap(i, e, c, j, cnt_ref):
        return (e, 0, jnp.where(act(i, e, c, cnt_ref), j, n_f - 1))

    a = pl.pallas_call(
        functools.partial(_moe_up_kernel, cap=cap, n_exp=n_exp, cw=_pick(d, (1024, 512, 256, 128))),
        grid_spec=pltpu.PrefetchScalarGridSpec(
            num_scalar_prefetch=1,
            grid=(n_i, n_exp, n_chunk, n_f),
            in_specs=[pl.BlockSpec((tm, d), lambda i, e, c, j, s: (i, 0)),
                      pl.BlockSpec((None, 1, tm), lambda i, e, c, j, s: (e, 0, i)),
                      pl.BlockSpec((None, d, tn), wmap),
                      pl.BlockSpec((None, d, tn), wmap)],
            out_specs=pl.BlockSpec((cap, tn), lambda i, e, c, j, s: ((i * n_exp + e) * n_chunk + c, j)),
            scratch_shapes=[pltpu.VMEM((cap, d), BF16)]),
        out_shape=jax.ShapeDtypeStruct((n_i * n_exp * n_chunk * cap, f), BF16),
        compiler_params=_params("parallel", "arbitrary", "arbitrary", "arbitrary"),
        name="moe_up",
    )(cnt, h2, rank_rows, w1.astype(BF16), w3.astype(BF16))

    def amap(i, n, e, c, cnt_ref):
        return ((i * n_exp + e) * n_chunk + jnp.where(act(i, e, c, cnt_ref), c, 0), 0)

    return pl.pallas_call(
        functools.partial(_moe_down_kernel, cap=cap, n_exp=n_exp, n_chunk=n_chunk),
        grid_spec=pltpu.PrefetchScalarGridSpec(
            num_scalar_prefetch=1,
            grid=(n_i, n_n, n_exp, n_chunk),
            in_specs=[pl.BlockSpec((cap, f), amap),
                      pl.BlockSpec((None, f, tn), lambda i, n, e, c, s: (e, 0, n)),
                      pl.BlockSpec((tm, LANE), lambda i, n, e, c, s: (i, 0)),
                      pl.BlockSpec((tm, LANE), lambda i, n, e, c, s: (i, 0)),
                      pl.BlockSpec((tm, tn), lambda i, n, e, c, s: (i, n))],
            out_specs=pl.BlockSpec((tm, tn), lambda i, n, e, c, s: (i, n)),
            scratch_shapes=[pltpu.VMEM((tm, tn), F32)]),
        out_shape=jax.ShapeDtypeStruct((m, d), F32),
        compiler_params=_params("parallel", "parallel", "arbitrary", "arbitrary"),
        name="moe_down",
    )(cnt, a, w2.astype(BF16), rank_lanes, gates, h_res)


def kernel(x, meta_tokens, rel_bias, norm_mix, w_in, b_forget, q_norm_a, k_norm_a, kv_norm_b, w_ukv_b, q_norm_b, k_norm_b, w_proj_a, w_proj_b, w_gate, b_gate, w_out, norm_ffn, dense_w1, dense_w3, dense_w2, router, moe_w1, moe_w3, moe_w2):
    b_sz, s_len, d = x.shape
    depth = norm_mix.shape[0]
    n_heads = d // (2 * HEAD_DIM)
    hw = n_heads * HEAD_DIM
    latent = kv_norm_b.shape[1]
    iqw = IDX_HEADS * IDX_DIM
    assert s_len % LANE == 0 and N_META <= LANE and n_heads % B_KV_HEADS == 0
    t_len = LANE + s_len
    m = b_sz * t_len
    topk = min(TOPK_MAX, s_len // 4)
    scale = HEAD_DIM ** -0.5

    sizes = (hw, hw, hw, n_heads, hw, latent, iqw, IDX_DIM, IDX_HEADS)
    offs = np.concatenate([[0], np.cumsum(sizes)])
    o_qa, o_ka, o_va, o_fa, o_qb, o_cb, o_qi, o_ki, o_wi = (int(v) for v in offs[:-1])

    meta = jnp.broadcast_to(meta_tokens[None].astype(F32), (b_sz, N_META, d))
    h_res = jnp.concatenate(
        [meta, jnp.zeros((b_sz, LANE - N_META, d), F32), x.astype(F32)], axis=1).reshape(m, d)

    bias = _bias_strips(rel_bias)

    for l in range(depth):
        w = w_in[l]
        w_qkq = jnp.concatenate([w[:, o_qa:o_qa + hw], w[:, o_ka:o_ka + hw], w[:, o_qb:o_qb + hw]],
                                axis=1).astype(BF16)
        w_va = w[:, o_va:o_va + hw].astype(BF16)
        w_qi = w[:, o_qi:o_qi + iqw].astype(BF16)
        w_cb = w[:, o_cb:o_cb + latent].astype(BF16)
        n_small = IDX_DIM + n_heads + IDX_HEADS
        assert n_small <= LANE and IDX_DIM + n_heads <= IDX_DIM + IDX_HEADS
        w_small = jnp.concatenate(
            [w[:, o_ki:o_ki + IDX_DIM], w[:, o_fa:o_fa + n_heads],
             jnp.zeros((d, IDX_HEADS - n_heads), F32), w[:, o_wi:o_wi + IDX_HEADS],
             jnp.zeros((d, LANE - IDX_DIM - 2 * IDX_HEADS), F32)], axis=1).astype(BF16)
        gains = jnp.concatenate([jnp.tile(q_norm_a[l] * scale, n_heads), jnp.tile(k_norm_a[l], n_heads),
                                 jnp.tile(q_norm_b[l] * scale, n_heads)]).reshape(1, 3 * hw).astype(F32)
        brow = jnp.zeros((1, LANE), F32).at[0, IDX_DIM:IDX_DIM + n_heads].set(b_forget[l].astype(F32))

        h = _rmsnorm(h_res, norm_mix[l])
        qkq = _matmul(h, w_qkq, _ep_headnorm,
                      [(gains, lambda tm, tn: (1, tn), lambda i, j, k: (0, j))], BF16, "proj_qkq")
        va = _matmul(h, w_va, _ep_plain, [], BF16, "proj_va")
        qi = _matmul(h, w_qi, _ep_plain, [], BF16, "proj_qi")
        cb = _matmul(h, w_cb, _ep_plain, [], BF16, "proj_cb")
        small = _matmul(h, w_small, _ep_plain, [], F32, "proj_small")
        gate = _matmul(h, w_gate[l].astype(BF16), _ep_gate,
                       [(b_gate[l].reshape(1, 2 * d).astype(F32), lambda tm, tn: (1, tn),
                         lambda i, j, k: (0, j))], BF16, "proj_gate")

        c, kit, kb, vb = _post(small, cb, brow, kv_norm_b[l].reshape(1, latent).astype(F32),
                               w_ukv_b[l].astype(BF16), k_norm_b[l].reshape(1, HEAD_DIM).astype(F32),
                               b_sz, t_len, n_heads)

        oa = _fox(qkq.reshape(b_sz, t_len, 3 * hw), va.reshape(b_sz, t_len, hw),
                  c.reshape(b_sz, n_heads, 1, t_len), b_sz, t_len, n_heads).reshape(m, hw)
        kw = B_KV_HEADS * HEAD_DIM
        ob = _dsa(qi, small, kit, qkq, kb.reshape(b_sz, t_len, kw), vb.reshape(b_sz, t_len, kw),
                  bias, b_sz, t_len, n_heads, topk)

        tm, tn, tk = _mm_tiles(m, d, hw)
        nj = d // tn
        xspec = pl.BlockSpec((tm, tk), lambda i, j, k: (i, k))
        wspec = pl.BlockSpec((tk, tn), lambda i, j, k: (k, j))
        merged = _mm([(oa, xspec), (ob, xspec)],
                     [(w_proj_a[l].astype(BF16), wspec), (w_proj_b[l].astype(BF16), wspec)], [0, 1],
                     [(gate, pl.BlockSpec((tm, tn), lambda i, j, k: (i, j))),
                      (gate, pl.BlockSpec((tm, tn), lambda i, j, k: (i, j + nj)))],
                     _ep_merge, jax.ShapeDtypeStruct((m, d), BF16),
                     pl.BlockSpec((tm, tn), lambda i, j, k: (i, j)),
                     (m // tm, nj, hw // tk), tm, tn, "merge_proj")
        h_res = _matmul(merged, w_out[l].astype(BF16), _ep_residual,
                        [(h_res, lambda tm, tn: (tm, tn), lambda i, j, k: (i, j))], F32, "out_proj")

        jj = l // 2
        if l % 2 == 0:
            h2 = _rmsnorm(h_res, norm_ffn[l])
            h_res = _dense_ffn(h2, h_res, dense_w1[jj], dense_w3[jj], dense_w2[jj])
        else:
            h2, gates = _norm_router(h_res, norm_ffn[l], router[jj])
            h_res = _moe_ffn(h2, gates, h_res, moe_w1[jj], moe_w3[jj], moe_w2[jj])

    return h_res.reshape(b_sz, t_len, d)[:, LANE:, :]
```

```python
import functools

import numpy as np
import jax
import jax.numpy as jnp
from jax import lax
from jax.experimental import pallas as pl
from jax.experimental.pallas import tpu as pltpu

F32 = jnp.float32
BF16 = jnp.bfloat16

CHUNK = 64
N_META = 16
HEAD_DIM = 128
B_KV_HEADS = 2
IDX_HEADS = 16
IDX_DIM = 64
TOPK_MAX = 256
REL_BUCKETS = 32
REL_MAX_DIST = 128
EPS = 1e-6
NEG = -1e30

LANE = 128
VMEM_LIMIT_BYTES = 56 * 1024 * 1024
INT_MIN = -2147483648
NEGINF_KEY = -2139095041


def _pick(n, prefs):
    for p in prefs:
        if n % p == 0:
            return p
    return n


def _params(*sem):
    return pltpu.CompilerParams(dimension_semantics=sem, vmem_limit_bytes=VMEM_LIMIT_BYTES)


def _split3(x):
    hi = x.astype(BF16)
    r1 = x - hi.astype(F32)
    mid = r1.astype(BF16)
    lo = (r1 - mid.astype(F32)).astype(BF16)
    return hi, mid, lo


def _dot(a, b):
    return jnp.dot(a, b, preferred_element_type=F32)


def _dot_nt(a, b):
    return lax.dot_general(a, b, (((1,), (1,)), ((), ())), preferred_element_type=F32)


def _sigmoid(x):
    return 1.0 / (1.0 + jnp.exp(-x))


def _rmsnorm_kernel(x_ref, g_ref, o_ref):
    x = x_ref[...]
    ms = jnp.mean(x * x, axis=-1, keepdims=True)
    o_ref[...] = (x * lax.rsqrt(ms + EPS) * g_ref[...]).astype(o_ref.dtype)


def _rmsnorm(x, g):
    m, d = x.shape
    tm = _pick(m, (256, 128))
    return pl.pallas_call(
        _rmsnorm_kernel,
        grid=(m // tm,),
        in_specs=[pl.BlockSpec((tm, d), lambda i: (i, 0)),
                  pl.BlockSpec((1, d), lambda i: (0, 0))],
        out_specs=pl.BlockSpec((tm, d), lambda i: (i, 0)),
        out_shape=jax.ShapeDtypeStruct((m, d), BF16),
        compiler_params=_params("parallel"),
        name="rmsnorm",
    )(x, g.reshape(1, d))


def _mm_kernel(*refs, n_x, n_w, pair_x, n_extra, nk, epilogue):
    xs = refs[:n_x]
    ws = refs[n_x:n_x + n_w]
    extras = refs[n_x + n_w:n_x + n_w + n_extra]
    o_ref = refs[n_x + n_w + n_extra]
    accs = refs[n_x + n_w + n_extra + 1:]

    def finish(vals):
        o_ref[...] = epilogue(vals, [e[...] for e in extras]).astype(o_ref.dtype)

    if nk == 1:
        finish([_dot(xs[pair_x[p]][...], ws[p][...]) for p in range(n_w)])
        return

    k = pl.program_id(2)
    for p in range(n_w):
        d = _dot(xs[pair_x[p]][...], ws[p][...])

        @pl.when(k == 0)
        def _(d=d, p=p):
            accs[p][...] = d

        @pl.when(k > 0)
        def _(d=d, p=p):
            accs[p][...] += d

    @pl.when(k == nk - 1)
    def _():
        finish([a[...] for a in accs])


def _mm(xs, ws, pair_x, extras, epilogue, out_shape, out_spec, grid, tm, tn, name):
    nk = grid[2]
    arrays = [a for a, _ in xs] + [a for a, _ in ws] + [a for a, _ in extras]
    specs = [s for _, s in xs] + [s for _, s in ws] + [s for _, s in extras]
    scratch = [] if nk == 1 else [pltpu.VMEM((tm, tn), F32) for _ in ws]
    return pl.pallas_call(
        functools.partial(_mm_kernel, n_x=len(xs), n_w=len(ws), pair_x=tuple(pair_x),
                          n_extra=len(extras), nk=nk, epilogue=epilogue),
        grid=grid,
        in_specs=specs,
        out_specs=out_spec,
        out_shape=out_shape,
        scratch_shapes=scratch,
        compiler_params=_params("parallel", "parallel", "arbitrary"),
        name=name,
    )(*arrays)


def _mm_tiles(m, n, k):
    tm = _pick(m, (1408, 1024, 768, 512, 384, 256, 128))
    tn = _pick(n, (1024, 512, 256, 128))
    tk = _pick(k, (1024, 512, 256, 128))
    return tm, tn, tk


def _matmul(x, w, epilogue, extras, out_dtype, name):
    m, kd = x.shape
    n = w.shape[1]
    tm, tn, tk = _mm_tiles(m, n, kd)
    ex = [(a, pl.BlockSpec(bs(tm, tn), im)) for a, bs, im in extras]
    return _mm(
        [(x, pl.BlockSpec((tm, tk), lambda i, j, k: (i, k)))],
        [(w, pl.BlockSpec((tk, tn), lambda i, j, k: (k, j)))],
        [0], ex, epilogue,
        jax.ShapeDtypeStruct((m, n), out_dtype),
        pl.BlockSpec((tm, tn), lambda i, j, k: (i, j)),
        (m // tm, n // tn, kd // tk), tm, tn, name)


def _ep_plain(vals, extras):
    return vals[0]


def _ep_headnorm(vals, extras):
    a, g = vals[0], extras[0]
    outs = []
    for c in range(a.shape[1] // LANE):
        blk = a[:, c * LANE:(c + 1) * LANE]
        ms = jnp.mean(blk * blk, axis=-1, keepdims=True)
        outs.append(blk * lax.rsqrt(ms + EPS) * g[:, c * LANE:(c + 1) * LANE])
    return jnp.concatenate(outs, axis=1)


def _ep_gate(vals, extras):
    return _sigmoid(vals[0] + extras[0])


def _ep_merge(vals, extras):
    return extras[0].astype(F32) * vals[0] + extras[1].astype(F32) * vals[1]


def _ep_residual(vals, extras):
    return extras[0] + vals[0]


def _ep_swiglu(vals, extras):
    a, b = vals
    return a * _sigmoid(a) * b


def _post_kernel(small_ref, cb_ref, brow_ref, kvg_ref, wukv_ref, kng_ref,
                 c_ref, kit_ref, kb_ref, vb_ref, carry_ref, *, tq, n_kv):
    i = pl.program_id(1)
    sm = small_ref[...]
    row = i * tq + lax.broadcasted_iota(jnp.int32, (tq, 1), 0)
    valid = (row < N_META) | (row >= LANE)

    z = sm + brow_ref[...]
    lf = jnp.minimum(z, 0.0) - jnp.log1p(jnp.exp(-jnp.abs(z)))
    lf = jnp.where(valid, lf, 0.0)
    r_i = lax.broadcasted_iota(jnp.int32, (tq, tq), 0)
    c_i = lax.broadcasted_iota(jnp.int32, (tq, tq), 1)
    tri = jnp.where(r_i >= c_i, 1.0, 0.0).astype(BF16)
    hi, mid, lo = _split3(lf)
    cs = _dot(tri, hi) + _dot(tri, mid) + _dot(tri, lo)

    @pl.when(i == 0)
    def _():
        carry_ref[...] = jnp.zeros_like(carry_ref)

    cs = cs + carry_ref[...]
    carry_ref[...] = cs[tq - 1:tq, :]
    cs = jnp.where(valid, cs, -NEG)
    c_ref[...] = cs.T[IDX_DIM:IDX_DIM + c_ref.shape[0], :]

    kit = sm.T[0:IDX_DIM, :].astype(BF16)
    kit_ref[0:IDX_DIM, :] = kit
    kit_ref[IDX_DIM:2 * IDX_DIM, :] = kit

    cbv = cb_ref[...].astype(F32)
    ms = jnp.mean(cbv * cbv, axis=-1, keepdims=True)
    kvn = (cbv * lax.rsqrt(ms + EPS) * kvg_ref[...]).astype(BF16)
    kv = _dot(kvn, wukv_ref[...])
    kw = n_kv * HEAD_DIM
    for g in range(n_kv):
        blk = kv[:, g * HEAD_DIM:(g + 1) * HEAD_DIM]
        msk = jnp.mean(blk * blk, axis=-1, keepdims=True)
        kb_ref[:, g * HEAD_DIM:(g + 1) * HEAD_DIM] = (
            blk * lax.rsqrt(msk + EPS) * kng_ref[...]).astype(BF16)
    vb_ref[...] = kv[:, kw:2 * kw].astype(BF16)


def _post(small, cb, brow, kvg, wukv, kng, b_sz, t_len, n_heads):
    m = small.shape[0]
    latent = cb.shape[1]
    tq = _pick(t_len, (384, 256, 128))
    nq = t_len // tq
    kw = B_KV_HEADS * HEAD_DIM
    rowmap = lambda b, i: (b * nq + i, 0)
    const = lambda b, i: (0, 0)
    return pl.pallas_call(
        functools.partial(_post_kernel, tq=tq, n_kv=B_KV_HEADS),
        grid=(b_sz, nq),
        in_specs=[pl.BlockSpec((tq, LANE), rowmap),
                  pl.BlockSpec((tq, latent), rowmap),
                  pl.BlockSpec((1, LANE), const),
                  pl.BlockSpec((1, latent), const),
                  pl.BlockSpec((latent, 2 * kw), const),
                  pl.BlockSpec((1, HEAD_DIM), const)],
        out_specs=[pl.BlockSpec((None, n_heads, tq), lambda b, i: (b, 0, i)),
                   pl.BlockSpec((None, 2 * IDX_DIM, tq), lambda b, i: (b, 0, i)),
                   pl.BlockSpec((tq, kw), rowmap),
                   pl.BlockSpec((tq, kw), rowmap)],
        out_shape=[jax.ShapeDtypeStruct((b_sz, n_heads, t_len), F32),
                   jax.ShapeDtypeStruct((b_sz, 2 * IDX_DIM, t_len), BF16),
                   jax.ShapeDtypeStruct((m, kw), BF16),
                   jax.ShapeDtypeStruct((m, kw), BF16)],
        scratch_shapes=[pltpu.VMEM((1, LANE), F32)],
        compiler_params=_params("parallel", "arbitrary"),
        name="post_proj",
    )(small, cb, brow, kvg, wukv, kng)


def _fox_kernel(q_ref, k_ref, v_ref, c_ref, o_ref, *, tq, hpb):
    i = pl.program_id(2)

    def step(j, carry, diagonal):
        off = pl.multiple_of(j * tq, tq)
        out = []
        for hh in range(hpb):
            m, l, acc = carry[hh]
            hs = slice(hh * HEAD_DIM, (hh + 1) * HEAD_DIM)
            s = _dot_nt(q_ref[:, hs], k_ref[pl.ds(off, tq), hs]) - c_ref[hh, :, pl.ds(off, tq)]
            if diagonal:
                r_i = lax.broadcasted_iota(jnp.int32, (tq, tq), 0)
                c_i = lax.broadcasted_iota(jnp.int32, (tq, tq), 1)
                s = jnp.where(c_i <= r_i, s, NEG)
            m_new = jnp.maximum(m, jnp.max(s, axis=-1, keepdims=True))
            alpha = jnp.exp(m - m_new)
            p = jnp.exp(s - m_new)
            l = alpha * l + jnp.sum(p, axis=-1, keepdims=True)
            acc = alpha * acc + _dot(p.astype(BF16), v_ref[pl.ds(off, tq), hs])
            out.append((m_new, l, acc))
        return tuple(out)

    init = tuple((jnp.full((tq, 1), -jnp.inf, F32), jnp.zeros((tq, 1), F32),
                  jnp.zeros((tq, HEAD_DIM), F32)) for _ in range(hpb))
    carry = lax.fori_loop(0, i, lambda j, c: step(j, c, False), init)
    carry = step(i, carry, True)
    for hh in range(hpb):
        _, l, acc = carry[hh]
        o_ref[:, hh * HEAD_DIM:(hh + 1) * HEAD_DIM] = (acc / l).astype(o_ref.dtype)


def _fox(qk, v, c, b_sz, t_len, n_heads):
    tq = _pick(t_len, (384, 256, 128))
    nq = t_len // tq
    hpb = _pick(n_heads, (4, 2, 1))
    nhb = n_heads // hpb
    hbw = hpb * HEAD_DIM
    return pl.pallas_call(
        functools.partial(_fox_kernel, tq=tq, hpb=hpb),
        grid=(b_sz, nhb, nq),
        in_specs=[pl.BlockSpec((None, tq, hbw), lambda b, h, i: (b, i, h)),
                  pl.BlockSpec((None, t_len, hbw), lambda b, h, i: (b, 0, nhb + h)),
                  pl.BlockSpec((None, t_len, hbw), lambda b, h, i: (b, 0, h)),
                  pl.BlockSpec((None, hpb, 1, t_len), lambda b, h, i: (b, h, 0, 0))],
        out_specs=pl.BlockSpec((None, tq, hbw), lambda b, h, i: (b, i, h)),
        out_shape=jax.ShapeDtypeStruct((b_sz, t_len, n_heads * HEAD_DIM), BF16),
        compiler_params=_params("parallel", "parallel", "parallel"),
        name="fox_attention",
    )(qk, qk, v, c)


def _rel_bucket_int(rel):
    nb = REL_BUCKETS // 2
    max_exact = nb // 2
    ret = nb if rel > 0 else 0
    n = abs(rel)
    if n < max_exact:
        return ret + n
    span = nb - max_exact
    k = 0
    while (n ** span) * (max_exact ** (k + 1)) >= (REL_MAX_DIST ** (k + 1)) * (max_exact ** span):
        k += 1
        if max_exact + k >= nb - 1:
            break
    return ret + min(max_exact + k, nb - 1)


def _bucket_strips():
    far = _rel_bucket_int(-(LANE + 1))
    assert far == _rel_bucket_int(-(10 ** 6))
    q = np.arange(LANE)[:, None]
    k = np.arange(LANE)[None, :]
    lut = {r: _rel_bucket_int(r) for r in range(-3 * LANE, 3 * LANE)}
    f = np.vectorize(lambda r: lut[int(r)])
    own = f(k - q)
    prev = f(k - q - LANE)
    meta1 = f(k - (q + N_META))
    farblk = np.full((LANE, LANE), far)
    strips = np.stack([np.concatenate([own, farblk], 1),
                       np.concatenate([meta1, own], 1),
                       np.concatenate([prev, own], 1)])
    return strips.reshape(3, 1, LANE * 2 * LANE).astype(np.int32), far


def _bias_kernel(b_ref, r_ref, o_ref, *, far):
    bk = b_ref[...]
    w = bk.shape[1]
    oh = jnp.where(lax.broadcasted_iota(jnp.int32, (LANE, w), 0) == bk, 1.0, 0.0).astype(BF16)
    rb = r_ref[...]
    rb = rb - rb[:, far:far + 1]
    hi, mid, lo = _split3(rb)
    o_ref[...] = _dot(hi, oh) + _dot(mid, oh) + _dot(lo, oh)


def _bias_strips(rel_bias):
    n_heads = rel_bias.shape[1]
    hp = max(16, n_heads)
    strips, far = _bucket_strips()
    w_all = strips.shape[2]
    wc = 4096
    rbt = jnp.zeros((hp, LANE), F32).at[:n_heads, :REL_BUCKETS].set(rel_bias.T.astype(F32))
    out = pl.pallas_call(
        functools.partial(_bias_kernel, far=far),
        grid=(3, w_all // wc),
        in_specs=[pl.BlockSpec((None, 1, wc), lambda s, c: (s, 0, c)),
                  pl.BlockSpec((hp, LANE), lambda s, c: (0, 0))],
        out_specs=pl.BlockSpec((None, hp, wc), lambda s, c: (s, 0, c)),
        out_shape=jax.ShapeDtypeStruct((3, hp, w_all), F32),
        compiler_params=_params("parallel", "parallel"),
        name="rel_bias_strips",
    )(jnp.asarray(strips), rbt)
    return out[:, :n_heads].reshape(3, n_heads, LANE, 2 * LANE)


def _dsa_kernel(qi_ref, small_ref, kit_ref, qb_ref, kb_ref, vb_ref, bias_ref, o_ref,
                key_sc, *, kc, topk, n_heads):
    j = pl.program_id(1)
    bpc = kc // LANE
    nch = j // bpc + 1
    rep = n_heads // B_KV_HEADS

    lane = lax.broadcasted_iota(jnp.int32, (LANE, LANE), 1)
    lo_m = jnp.where(lane < IDX_DIM, 1.0, 0.0).astype(BF16)
    hi_m = jnp.where(lane >= IDX_DIM, 1.0, 0.0).astype(BF16)
    parts = []
    for p in range(IDX_HEADS // 2):
        blk = qi_ref[:, p * LANE:(p + 1) * LANE]
        parts.append(blk * lo_m)
        parts.append(blk * hi_m)
    lhs = jnp.concatenate(parts, axis=0)
    w_off = IDX_DIM + IDX_HEADS
    w = small_ref[:, w_off:w_off + IDX_HEADS] * ((IDX_HEADS ** -0.5) * (IDX_DIM ** -0.5))
    qrow = j * LANE + lax.broadcasted_iota(jnp.int32, (LANE, kc), 0)

    def idx_body(c, _):
        off = pl.multiple_of(c * kc, kc)
        d = _dot(lhs, kit_ref[:, pl.ds(off, kc)])
        acc = jnp.zeros((LANE, kc), F32)
        for h in range(IDX_HEADS):
            acc = acc + w[:, h:h + 1] * jnp.maximum(d[h * LANE:(h + 1) * LANE], 0.0)
        krow = off + lax.broadcasted_iota(jnp.int32, (LANE, kc), 1)
        adm = (krow < N_META) | ((krow >= LANE) & ((krow // CHUNK) <= (qrow // CHUNK)))
        isc = jnp.where(adm, acc, -jnp.inf)
        bits = pltpu.bitcast(isc, jnp.int32)
        key_sc[:, pl.ds(off, kc)] = bits ^ ((bits >> 31) & 0x7FFFFFFF)
        return 0

    lax.fori_loop(0, nch, idx_body, 0)

    def search(b, t_u):
        bit = jnp.left_shift(jnp.int32(1), 31 - b)
        cand_u = t_u | bit
        cand_s = cand_u ^ INT_MIN

        def cnt_body(c, cnt):
            off = pl.multiple_of(c * kc, kc)
            ge = jnp.where(key_sc[:, pl.ds(off, kc)] >= cand_s, 1, 0)
            for t in range(bpc):
                cnt = cnt + ge[:, t * LANE:(t + 1) * LANE]
            return cnt

        cnt = lax.fori_loop(0, nch, cnt_body, jnp.zeros((LANE, LANE), jnp.int32))
        tot = jnp.sum(cnt, axis=-1, keepdims=True)
        return jnp.where(tot >= topk, cand_u, t_u)

    t_u = lax.fori_loop(0, 32, search, jnp.zeros((LANE, 1), jnp.int32))
    thr = jnp.maximum(t_u ^ INT_MIN, NEGINF_KEY + 1)

    near = pl.multiple_of(jnp.maximum(j - 1, 0) * LANE, LANE)
    nfar = jnp.maximum(j - 2 + bpc, 0) // bpc
    far_limit = (j - 1) * LANE

    qgs = [jnp.concatenate(
        [qb_ref[:, (g * rep + r) * HEAD_DIM:(g * rep + r + 1) * HEAD_DIM] for r in range(rep)],
        axis=0) for g in range(B_KV_HEADS)]

    def attend(off, width, with_bias, limit, carries):
        sel = key_sc[:, pl.ds(off, width)] >= thr
        if limit is not None:
            krow = off + lax.broadcasted_iota(jnp.int32, (LANE, width), 1)
            sel = sel & (krow < limit)
        out = []
        for g in range(B_KV_HEADS):
            m, l, acc = carries[g]
            gsl = slice(g * HEAD_DIM, (g + 1) * HEAD_DIM)
            s = _dot_nt(qgs[g], kb_ref[pl.ds(off, width), gsl]).reshape(rep, LANE, width)
            if with_bias:
                s = s + bias_ref[g * rep:(g + 1) * rep]
            s = jnp.where(sel[None], s, NEG)
            m_new = jnp.maximum(m, jnp.max(s, axis=-1, keepdims=True))
            alpha = jnp.exp(m - m_new)
            p = jnp.exp(s - m_new)
            l = alpha * l + jnp.sum(p, axis=-1, keepdims=True)
            pv = _dot(p.reshape(rep * LANE, width).astype(BF16), vb_ref[pl.ds(off, width), gsl])
            acc = alpha * acc + pv.reshape(rep, LANE, HEAD_DIM)
            out.append((m_new, l, acc))
        return tuple(out)

    init = tuple((jnp.full((rep, LANE, 1), -jnp.inf, F32), jnp.zeros((rep, LANE, 1), F32),
                  jnp.zeros((rep, LANE, HEAD_DIM), F32)) for _ in range(B_KV_HEADS))
    carries = attend(near, 2 * LANE, True, None, init)

    def far_body(c, carries):
        return attend(pl.multiple_of(c * kc, kc), kc, False, far_limit, carries)

    carries = lax.fori_loop(0, nfar, far_body, carries)
    for g in range(B_KV_HEADS):
        _, l, acc = carries[g]
        out = acc / l
        for r in range(rep):
            o_ref[:, (g * rep + r) * HEAD_DIM:(g * rep + r + 1) * HEAD_DIM] = out[r].astype(o_ref.dtype)


def _dsa(qi, small, kit, qk, kb, vb, bias, b_sz, t_len, n_heads, topk):
    nb = t_len // LANE
    kc = _pick(t_len, (384, 256))
    assert kc >= topk and t_len >= 2 * LANE
    hw = n_heads * HEAD_DIM
    kw = B_KV_HEADS * HEAD_DIM
    rowmap = lambda b, j: (b * nb + j, 0)
    return pl.pallas_call(
        functools.partial(_dsa_kernel, kc=kc, topk=topk, n_heads=n_heads),
        grid=(b_sz, nb),
        in_specs=[pl.BlockSpec((LANE, IDX_HEADS * IDX_DIM), rowmap),
                  pl.BlockSpec((LANE, LANE), rowmap),
                  pl.BlockSpec((None, 2 * IDX_DIM, t_len), lambda b, j: (b, 0, 0)),
                  pl.BlockSpec((LANE, hw), lambda b, j: (b * nb + j, 2)),
                  pl.BlockSpec((None, t_len, kw), lambda b, j: (b, 0, 0)),
                  pl.BlockSpec((None, t_len, kw), lambda b, j: (b, 0, 0)),
                  pl.BlockSpec((None, n_heads, LANE, 2 * LANE),
                               lambda b, j: (jnp.minimum(j, 2), 0, 0, 0))],
        out_specs=pl.BlockSpec((LANE, hw), rowmap),
        out_shape=jax.ShapeDtypeStruct((qi.shape[0], hw), BF16),
        scratch_shapes=[pltpu.VMEM((LANE, t_len), jnp.int32)],
        compiler_params=_params("parallel", "parallel"),
        name="dsa_attention",
    )(qi, small, kit, qk, kb, vb, bias)


def _norm_router_kernel(x_ref, g_ref, r_ref, o_ref, gate_ref, *, n_exp):
    x = x_ref[...]
    ms = jnp.mean(x * x, axis=-1, keepdims=True)
    y = x * lax.rsqrt(ms + EPS) * g_ref[...]
    o_ref[...] = y.astype(o_ref.dtype)
    logits = [jnp.sum(y * r_ref[e:e + 1, :], axis=-1, keepdims=True) for e in range(n_exp)]

    def argmax(vals):
        best, bi = vals[0], jnp.zeros_like(vals[0], dtype=jnp.int32)
        for e in range(1, n_exp):
            upd = vals[e] > best
            best = jnp.where(upd, vals[e], best)
            bi = jnp.where(upd, e, bi)
        return best, bi

    m1, i1 = argmax(logits)
    m2, i2 = argmax([jnp.where(i1 == e, -jnp.inf, logits[e]) for e in range(n_exp)])
    t = jnp.exp(m2 - m1)
    w1 = 1.0 / (1.0 + t)
    w2 = t / (1.0 + t)
    lane = lax.broadcasted_iota(jnp.int32, gate_ref.shape, 1)
    gate_ref[...] = jnp.where(lane == i1, w1, jnp.where(lane == i2, w2, 0.0))


def _norm_router(x, g, router):
    m, d = x.shape
    n_exp = router.shape[1]
    tm = _pick(m, (256, 128))
    return pl.pallas_call(
        functools.partial(_norm_router_kernel, n_exp=n_exp),
        grid=(m // tm,),
        in_specs=[pl.BlockSpec((tm, d), lambda i: (i, 0)),
                  pl.BlockSpec((1, d), lambda i: (0, 0)),
                  pl.BlockSpec((n_exp, d), lambda i: (0, 0))],
        out_specs=[pl.BlockSpec((tm, d), lambda i: (i, 0)),
                   pl.BlockSpec((tm, LANE), lambda i: (i, 0))],
        out_shape=[jax.ShapeDtypeStruct((m, d), BF16),
                   jax.ShapeDtypeStruct((m, LANE), F32)],
        compiler_params=_params("parallel"),
        name="norm_router",
    )(x, g.reshape(1, d), router.T.astype(F32))


def _dense_ffn(h2, h_res, w1, w3, w2):
    m, d = h2.shape
    f = w1.shape[1]
    fp = ((f + 1023) // 1024) * 1024 if f > 1024 else f
    pad = ((0, 0), (0, fp - f))
    w1p = jnp.pad(w1, pad).astype(BF16)
    w3p = jnp.pad(w3, pad).astype(BF16)
    w2p = jnp.pad(w2, ((0, fp - f), (0, 0))).astype(BF16)
    tm, tn, tk = _mm_tiles(m, fp, d)
    wspec = pl.BlockSpec((tk, tn), lambda i, j, k: (k, j))
    a = _mm([(h2, pl.BlockSpec((tm, tk), lambda i, j, k: (i, k)))],
            [(w1p, wspec), (w3p, wspec)], [0, 0], [], _ep_swiglu,
            jax.ShapeDtypeStruct((m, fp), BF16),
            pl.BlockSpec((tm, tn), lambda i, j, k: (i, j)),
            (m // tm, fp // tn, d // tk), tm, tn, "ffn_up")
    return _matmul(a, w2p, _ep_residual,
                   [(h_res, lambda tm, tn: (tm, tn), lambda i, j, k: (i, j))], F32, "ffn_down")


MOE_CAP = 384


def _moe_up_kernel(cnt_ref, h_ref, rr_ref, w1_ref, w3_ref, a_ref, xs_sc, *, cap, n_exp, cw):
    i, e, c, f = (pl.program_id(a) for a in range(4))
    active = (c == 0) | (c * cap < cnt_ref[i * n_exp + e])

    @pl.when(active & (f == 0))
    def _():
        rr = rr_ref[...]
        sel = jnp.where(rr - c * cap == lax.broadcasted_iota(jnp.int32, (cap, rr.shape[1]), 0),
                        1.0, 0.0).astype(BF16)
        for cc in range(h_ref.shape[1] // cw):
            xs_sc[:, cc * cw:(cc + 1) * cw] = _dot(sel, h_ref[:, cc * cw:(cc + 1) * cw]).astype(BF16)

    @pl.when(active)
    def _():
        xs = xs_sc[...]
        a = _dot(xs, w1_ref[...])
        b = _dot(xs, w3_ref[...])
        a_ref[...] = (a * _sigmoid(a) * b).astype(a_ref.dtype)

    @pl.when(jnp.logical_not(active))
    def _():
        a_ref[...] = jnp.zeros_like(a_ref)


def _moe_down_kernel(cnt_ref, a_ref, w2_ref, rk_ref, g_ref, res_ref, o_ref, acc, *, cap, n_exp, n_chunk):
    i, e, c = pl.program_id(0), pl.program_id(2), pl.program_id(3)
    active = (c == 0) | (c * cap < cnt_ref[i * n_exp + e])

    @pl.when((e == 0) & (c == 0))
    def _():
        acc[...] = res_ref[...]

    @pl.when(active)
    def _():
        y = _dot(a_ref[...], w2_ref[...]).astype(BF16)
        lane = lax.broadcasted_iota(jnp.int32, rk_ref.shape, 1)
        rank = jnp.sum(jnp.where(lane == e, rk_ref[...], 0.0), axis=-1, keepdims=True)
        gate = jnp.sum(jnp.where(lane == e, g_ref[...], 0.0), axis=-1, keepdims=True)
        tm = rk_ref.shape[0]
        scat = jnp.where(rank.astype(jnp.int32) - c * cap
                         == lax.broadcasted_iota(jnp.int32, (tm, cap), 1), 1.0, 0.0).astype(BF16)
        acc[...] += gate * _dot(scat, y)

    @pl.when((e == n_exp - 1) & (c == n_chunk - 1))
    def _():
        o_ref[...] = acc[...]


def _moe_ffn(h2, gates, h_res, w1, w3, w2):
    m, d = h2.shape
    n_exp, _, f = w1.shape
    tm = _pick(m, (1408, 768, 512, 256, 128))
    tn = _pick(f, (512, 256, 128))
    cap = min(MOE_CAP, tm)
    n_chunk = -(-tm // cap)
    n_i, n_f, n_n = m // tm, f // tn, d // tn

    sel = (gates[:, :n_exp] > 0).reshape(n_i, tm, n_exp)
    seli = sel.astype(jnp.int32)
    rank = jnp.where(sel, jnp.cumsum(seli, axis=1) - seli, -1).reshape(m, n_exp)
    cnt = jnp.sum(seli, axis=1).reshape(n_i * n_exp)
    rank_lanes = jnp.full((m, LANE), -1.0, F32).at[:, :n_exp].set(rank.astype(F32))
    rank_rows = rank.T.reshape(n_exp, 1, m)

    def act(i, e, c, cnt_ref):
        return (c == 0) | (c * cap < cnt_ref[i * n_exp + e])

    def wmap(i, e, c, j, cnt_ref):
        return (e, 0, jnp.where(act(i, e, c, cnt_ref), j, n_f - 1))

    a = pl.pallas_call(
        functools.partial(_moe_up_kernel, cap=cap, n_exp=n_exp, cw=_pick(d, (1024, 512, 256, 128))),
        grid_spec=pltpu.PrefetchScalarGridSpec(
            num_scalar_prefetch=1,
            grid=(n_i, n_exp, n_chunk, n_f),
            in_specs=[pl.BlockSpec((tm, d), lambda i, e, c, j, s: (i, 0)),
                      pl.BlockSpec((None, 1, tm), lambda i, e, c, j, s: (e, 0, i)),
                      pl.BlockSpec((None, d, tn), wmap),
                      pl.BlockSpec((None, d, tn), wmap)],
            out_specs=pl.BlockSpec((cap, tn), lambda i, e, c, j, s: ((i * n_exp + e) * n_chunk + c, j)),
            scratch_shapes=[pltpu.VMEM((cap, d), BF16)]),
        out_shape=jax.ShapeDtypeStruct((n_i * n_exp * n_chunk * cap, f), BF16),
        compiler_params=_params("parallel", "arbitrary", "arbitrary", "arbitrary"),
        name="moe_up",
    )(cnt, h2, rank_rows, w1.astype(BF16), w3.astype(BF16))

    def amap(i, n, e, c, cnt_ref):
        return ((i * n_exp + e) * n_chunk + jnp.where(act(i, e, c, cnt_ref), c, 0), 0)

    return pl.pallas_call(
        functools.partial(_moe_down_kernel, cap=cap, n_exp=n_exp, n_chunk=n_chunk),
        grid_spec=pltpu.PrefetchScalarGridSpec(
            num_scalar_prefetch=1,
            grid=(n_i, n_n, n_exp, n_chunk),
            in_specs=[pl.BlockSpec((cap, f), amap),
                      pl.BlockSpec((None, f, tn), lambda i, n, e, c, s: (e, 0, n)),
                      pl.BlockSpec((tm, LANE), lambda i, n, e, c, s: (i, 0)),
                      pl.BlockSpec((tm, LANE), lambda i, n, e, c, s: (i, 0)),
                      pl.BlockSpec((tm, tn), lambda i, n, e, c, s: (i, n))],
            out_specs=pl.BlockSpec((tm, tn), lambda i, n, e, c, s: (i, n)),
            scratch_shapes=[pltpu.VMEM((tm, tn), F32)]),
        out_shape=jax.ShapeDtypeStruct((m, d), F32),
        compiler_params=_params("parallel", "parallel", "arbitrary", "arbitrary"),
        name="moe_down",
    )(cnt, a, w2.astype(BF16), rank_lanes, gates, h_res)


def kernel(x, meta_tokens, rel_bias, norm_mix, w_in, b_forget, q_norm_a, k_norm_a, kv_norm_b, w_ukv_b, q_norm_b, k_norm_b, w_proj_a, w_proj_b, w_gate, b_gate, w_out, norm_ffn, dense_w1, dense_w3, dense_w2, router, moe_w1, moe_w3, moe_w2):
    b_sz, s_len, d = x.shape
    depth = norm_mix.shape[0]
    n_heads = d // (2 * HEAD_DIM)
    hw = n_heads * HEAD_DIM
    latent = kv_norm_b.shape[1]
    iqw = IDX_HEADS * IDX_DIM
    assert s_len % LANE == 0 and N_META <= LANE and n_heads % B_KV_HEADS == 0
    t_len = LANE + s_len
    m = b_sz * t_len
    topk = min(TOPK_MAX, s_len // 4)
    scale = HEAD_DIM ** -0.5

    sizes = (hw, hw, hw, n_heads, hw, latent, iqw, IDX_DIM, IDX_HEADS)
    offs = np.concatenate([[0], np.cumsum(sizes)])
    o_qa, o_ka, o_va, o_fa, o_qb, o_cb, o_qi, o_ki, o_wi = (int(v) for v in offs[:-1])

    meta = jnp.broadcast_to(meta_tokens[None].astype(F32), (b_sz, N_META, d))
    h_res = jnp.concatenate(
        [meta, jnp.zeros((b_sz, LANE - N_META, d), F32), x.astype(F32)], axis=1).reshape(m, d)

    bias = _bias_strips(rel_bias)

    for l in range(depth):
        w = w_in[l]
        w_qkq = jnp.concatenate([w[:, o_qa:o_qa + hw], w[:, o_ka:o_ka + hw], w[:, o_qb:o_qb + hw]],
                                axis=1).astype(BF16)
        w_va = w[:, o_va:o_va + hw].astype(BF16)
        w_qi = w[:, o_qi:o_qi + iqw].astype(BF16)
        w_cb = w[:, o_cb:o_cb + latent].astype(BF16)
        n_small = IDX_DIM + n_heads + IDX_HEADS
        assert n_small <= LANE and IDX_DIM + n_heads <= IDX_DIM + IDX_HEADS
        w_small = jnp.concatenate(
            [w[:, o_ki:o_ki + IDX_DIM], w[:, o_fa:o_fa + n_heads],
             jnp.zeros((d, IDX_HEADS - n_heads), F32), w[:, o_wi:o_wi + IDX_HEADS],
             jnp.zeros((d, LANE - IDX_DIM - 2 * IDX_HEADS), F32)], axis=1).astype(BF16)
        gains = jnp.concatenate([jnp.tile(q_norm_a[l] * scale, n_heads), jnp.tile(k_norm_a[l], n_heads),
                                 jnp.tile(q_norm_b[l] * scale, n_heads)]).reshape(1, 3 * hw).astype(F32)
        brow = jnp.zeros((1, LANE), F32).at[0, IDX_DIM:IDX_DIM + n_heads].set(b_forget[l].astype(F32))

        h = _rmsnorm(h_res, norm_mix[l])
        qkq = _matmul(h, w_qkq, _ep_headnorm,
                      [(gains, lambda tm, tn: (1, tn), lambda i, j, k: (0, j))], BF16, "proj_qkq")
        va = _matmul(h, w_va, _ep_plain, [], BF16, "proj_va")
        qi = _matmul(h, w_qi, _ep_plain, [], BF16, "proj_qi")
        cb = _matmul(h, w_cb, _ep_plain, [], BF16, "proj_cb")
        small = _matmul(h, w_small, _ep_plain, [], F32, "proj_small")
        gate = _matmul(h, w_gate[l].astype(BF16), _ep_gate,
                       [(b_gate[l].reshape(1, 2 * d).astype(F32), lambda tm, tn: (1, tn),
                         lambda i, j, k: (0, j))], BF16, "proj_gate")

        c, kit, kb, vb = _post(small, cb, brow, kv_norm_b[l].reshape(1, latent).astype(F32),
                               w_ukv_b[l].astype(BF16), k_norm_b[l].reshape(1, HEAD_DIM).astype(F32),
                               b_sz, t_len, n_heads)

        oa = _fox(qkq.reshape(b_sz, t_len, 3 * hw), va.reshape(b_sz, t_len, hw),
                  c.reshape(b_sz, n_heads, 1, t_len), b_sz, t_len, n_heads).reshape(m, hw)
        kw = B_KV_HEADS * HEAD_DIM
        ob = _dsa(qi, small, kit, qkq, kb.reshape(b_sz, t_len, kw), vb.reshape(b_sz, t_len, kw),
                  bias, b_sz, t_len, n_heads, topk)

        tm, tn, tk = _mm_tiles(m, d, hw)
        nj = d // tn
        xspec = pl.BlockSpec((tm, tk), lambda i, j, k: (i, k))
        wspec = pl.BlockSpec((tk, tn), lambda i, j, k: (k, j))
        merged = _mm([(oa, xspec), (ob, xspec)],
                     [(w_proj_a[l].astype(BF16), wspec), (w_proj_b[l].astype(BF16), wspec)], [0, 1],
                     [(gate, pl.BlockSpec((tm, tn), lambda i, j, k: (i, j))),
                      (gate, pl.BlockSpec((tm, tn), lambda i, j, k: (i, j + nj)))],
                     _ep_merge, jax.ShapeDtypeStruct((m, d), BF16),
                     pl.BlockSpec((tm, tn), lambda i, j, k: (i, j)),
                     (m // tm, nj, hw // tk), tm, tn, "merge_proj")
        h_res = _matmul(merged, w_out[l].astype(BF16), _ep_residual,
                        [(h_res, lambda tm, tn: (tm, tn), lambda i, j, k: (i, j))], F32, "out_proj")

        jj = l // 2
        if l % 2 == 0:
            h2 = _rmsnorm(h_res, norm_ffn[l])
            h_res = _dense_ffn(h2, h_res, dense_w1[jj], dense_w3[jj], dense_w2[jj])
        else:
            h2, gates = _norm_router(h_res, norm_ffn[l], router[jj])
            h_res = _moe_ffn(h2, gates, h_res, moe_w1[jj], moe_w3[jj], moe_w2[jj])

    return h_res.reshape(b_sz, t_len, d)[:, LANE:, :]
```

```python
import functools

import numpy as np
import jax
import jax.numpy as jnp
from jax import lax
from jax.experimental import pallas as pl
from jax.experimental.pallas import tpu as pltpu

F32 = jnp.float32
BF16 = jnp.bfloat16

CHUNK = 64
N_META = 16
HEAD_DIM = 128
B_KV_HEADS = 2
IDX_HEADS = 16
IDX_DIM = 64
TOPK_MAX = 256
REL_BUCKETS = 32
REL_MAX_DIST = 128
EPS = 1e-6
NEG = -1e30

LANE = 128
VMEM_LIMIT_BYTES = 56 * 1024 * 1024
INT_MIN = -2147483648
NEGINF_KEY = -2139095041


def _pick(n, prefs):
    for p in prefs:
        if n % p == 0:
            return p
    return n


def _params(*sem):
    return pltpu.CompilerParams(dimension_semantics=sem, vmem_limit_bytes=VMEM_LIMIT_BYTES)


def _split3(x):
    hi = x.astype(BF16)
    r1 = x - hi.astype(F32)
    mid = r1.astype(BF16)
    lo = (r1 - mid.astype(F32)).astype(BF16)
    return hi, mid, lo


def _dot(a, b):
    return jnp.dot(a, b, preferred_element_type=F32)


def _dot_nt(a, b):
    return lax.dot_general(a, b, (((1,), (1,)), ((), ())), preferred_element_type=F32)


def _sigmoid(x):
    return 1.0 / (1.0 + jnp.exp(-x))


def _rmsnorm_kernel(x_ref, g_ref, o_ref):
    x = x_ref[...]
    ms = jnp.mean(x * x, axis=-1, keepdims=True)
    o_ref[...] = (x * lax.rsqrt(ms + EPS) * g_ref[...]).astype(o_ref.dtype)


def _rmsnorm(x, g):
    m, d = x.shape
    tm = _pick(m, (256, 128))
    return pl.pallas_call(
        _rmsnorm_kernel,
        grid=(m // tm,),
        in_specs=[pl.BlockSpec((tm, d), lambda i: (i, 0)),
                  pl.BlockSpec((1, d), lambda i: (0, 0))],
        out_specs=pl.BlockSpec((tm, d), lambda i: (i, 0)),
        out_shape=jax.ShapeDtypeStruct((m, d), BF16),
        compiler_params=_params("parallel"),
        name="rmsnorm",
    )(x, g.reshape(1, d))


def _mm_kernel(*refs, n_x, n_w, pair_x, n_extra, nk, epilogue):
    xs = refs[:n_x]
    ws = refs[n_x:n_x + n_w]
    extras = refs[n_x + n_w:n_x + n_w + n_extra]
    o_ref = refs[n_x + n_w + n_extra]
    accs = refs[n_x + n_w + n_extra + 1:]

    def finish(vals):
        o_ref[...] = epilogue(vals, [e[...] for e in extras]).astype(o_ref.dtype)

    if nk == 1:
        finish([_dot(xs[pair_x[p]][...], ws[p][...]) for p in range(n_w)])
        return

    k = pl.program_id(2)
    for p in range(n_w):
        d = _dot(xs[pair_x[p]][...], ws[p][...])

        @pl.when(k == 0)
        def _(d=d, p=p):
            accs[p][...] = d

        @pl.when(k > 0)
        def _(d=d, p=p):
            accs[p][...] += d

    @pl.when(k == nk - 1)
    def _():
        finish([a[...] for a in accs])


def _mm(xs, ws, pair_x, extras, epilogue, out_shape, out_spec, grid, tm, tn, name):
    nk = grid[2]
    arrays = [a for a, _ in xs] + [a for a, _ in ws] + [a for a, _ in extras]
    specs = [s for _, s in xs] + [s for _, s in ws] + [s for _, s in extras]
    scratch = [] if nk == 1 else [pltpu.VMEM((tm, tn), F32) for _ in ws]
    return pl.pallas_call(
        functools.partial(_mm_kernel, n_x=len(xs), n_w=len(ws), pair_x=tuple(pair_x),
                          n_extra=len(extras), nk=nk, epilogue=epilogue),
        grid=grid,
        in_specs=specs,
        out_specs=out_spec,
        out_shape=out_shape,
        scratch_shapes=scratch,
        compiler_params=_params("parallel", "parallel", "arbitrary"),
        name=name,
    )(*arrays)


def _mm_tiles(m, n, k):
    tm = _pick(m, (1408, 1024, 768, 512, 384, 256, 128))
    tn = _pick(n, (1024, 512, 256, 128))
    tk = _pick(k, (1024, 512, 256, 128))
    return tm, tn, tk


def _matmul(x, w, epilogue, extras, out_dtype, name):
    m, kd = x.shape
    n = w.shape[1]
    tm, tn, tk = _mm_tiles(m, n, kd)
    ex = [(a, pl.BlockSpec(bs(tm, tn), im)) for a, bs, im in extras]
    return _mm(
        [(x, pl.BlockSpec((tm, tk), lambda i, j, k: (i, k)))],
        [(w, pl.BlockSpec((tk, tn), lambda i, j, k: (k, j)))],
        [0], ex, epilogue,
        jax.ShapeDtypeStruct((m, n), out_dtype),
        pl.BlockSpec((tm, tn), lambda i, j, k: (i, j)),
        (m // tm, n // tn, kd // tk), tm, tn, name)


def _ep_plain(vals, extras):
    return vals[0]


def _ep_headnorm(vals, extras):
    a, g = vals[0], extras[0]
    outs = []
    for c in range(a.shape[1] // LANE):
        blk = a[:, c * LANE:(c + 1) * LANE]
        ms = jnp.mean(blk * blk, axis=-1, keepdims=True)
        outs.append(blk * lax.rsqrt(ms + EPS) * g[:, c * LANE:(c + 1) * LANE])
    return jnp.concatenate(outs, axis=1)


def _ep_gate(vals, extras):
    return _sigmoid(vals[0] + extras[0])


def _ep_merge(vals, extras):
    return extras[0].astype(F32) * vals[0] + extras[1].astype(F32) * vals[1]


def _ep_residual(vals, extras):
    return extras[0] + vals[0]


def _ep_swiglu(vals, extras):
    a, b = vals
    return a * _sigmoid(a) * b


def _post_kernel(small_ref, cb_ref, brow_ref, kvg_ref, wukv_ref, kng_ref,
                 c_ref, kit_ref, kb_ref, vb_ref, carry_ref, *, tq, n_kv):
    i = pl.program_id(1)
    sm = small_ref[...]
    row = i * tq + lax.broadcasted_iota(jnp.int32, (tq, 1), 0)
    valid = (row < N_META) | (row >= LANE)

    z = sm + brow_ref[...]
    lf = jnp.minimum(z, 0.0) - jnp.log1p(jnp.exp(-jnp.abs(z)))
    lf = jnp.where(valid, lf, 0.0)
    r_i = lax.broadcasted_iota(jnp.int32, (tq, tq), 0)
    c_i = lax.broadcasted_iota(jnp.int32, (tq, tq), 1)
    tri = jnp.where(r_i >= c_i, 1.0, 0.0).astype(BF16)
    hi, mid, lo = _split3(lf)
    cs = _dot(tri, hi) + _dot(tri, mid) + _dot(tri, lo)

    @pl.when(i == 0)
    def _():
        carry_ref[...] = jnp.zeros_like(carry_ref)

    cs = cs + carry_ref[...]
    carry_ref[...] = cs[tq - 1:tq, :]
    cs = jnp.where(valid, cs, -NEG)
    c_ref[...] = cs.T[IDX_DIM:IDX_DIM + c_ref.shape[0], :]

    kit = sm.T[0:IDX_DIM, :].astype(BF16)
    kit_ref[0:IDX_DIM, :] = kit
    kit_ref[IDX_DIM:2 * IDX_DIM, :] = kit

    cbv = cb_ref[...].astype(F32)
    ms = jnp.mean(cbv * cbv, axis=-1, keepdims=True)
    kvn = (cbv * lax.rsqrt(ms + EPS) * kvg_ref[...]).astype(BF16)
    kv = _dot(kvn, wukv_ref[...])
    kw = n_kv * HEAD_DIM
    for g in range(n_kv):
        blk = kv[:, g * HEAD_DIM:(g + 1) * HEAD_DIM]
        msk = jnp.mean(blk * blk, axis=-1, keepdims=True)
        kb_ref[:, g * HEAD_DIM:(g + 1) * HEAD_DIM] = (
            blk * lax.rsqrt(msk + EPS) * kng_ref[...]).astype(BF16)
    vb_ref[...] = kv[:, kw:2 * kw].astype(BF16)


def _post(small, cb, brow, kvg, wukv, kng, b_sz, t_len, n_heads):
    m = small.shape[0]
    latent = cb.shape[1]
    tq = _pick(t_len, (384, 256, 128))
    nq = t_len // tq
    kw = B_KV_HEADS * HEAD_DIM
    rowmap = lambda b, i: (b * nq + i, 0)
    const = lambda b, i: (0, 0)
    return pl.pallas_call(
        functools.partial(_post_kernel, tq=tq, n_kv=B_KV_HEADS),
        grid=(b_sz, nq),
        in_specs=[pl.BlockSpec((tq, LANE), rowmap),
                  pl.BlockSpec((tq, latent), rowmap),
                  pl.BlockSpec((1, LANE), const),
                  pl.BlockSpec((1, latent), const),
                  pl.BlockSpec((latent, 2 * kw), const),
                  pl.BlockSpec((1, HEAD_DIM), const)],
        out_specs=[pl.BlockSpec((None, n_heads, tq), lambda b, i: (b, 0, i)),
                   pl.BlockSpec((None, 2 * IDX_DIM, tq), lambda b, i: (b, 0, i)),
                   pl.BlockSpec((tq, kw), rowmap),
                   pl.BlockSpec((tq, kw), rowmap)],
        out_shape=[jax.ShapeDtypeStruct((b_sz, n_heads, t_len), F32),
                   jax.ShapeDtypeStruct((b_sz, 2 * IDX_DIM, t_len), BF16),
                   jax.ShapeDtypeStruct((m, kw), BF16),
                   jax.ShapeDtypeStruct((m, kw), BF16)],
        scratch_shapes=[pltpu.VMEM((1, LANE), F32)],
        compiler_params=_params("parallel", "arbitrary"),
        name="post_proj",
    )(small, cb, brow, kvg, wukv, kng)


def _fox_kernel(q_ref, k_ref, v_ref, c_ref, o_ref, *, tq, hpb):
    i = pl.program_id(2)

    def step(j, carry, diagonal):
        off = pl.multiple_of(j * tq, tq)
        out = []
        for hh in range(hpb):
            m, l, acc = carry[hh]
            hs = slice(hh * HEAD_DIM, (hh + 1) * HEAD_DIM)
            s = _dot_nt(q_ref[:, hs], k_ref[pl.ds(off, tq), hs]) - c_ref[hh, :, pl.ds(off, tq)]
            if diagonal:
                r_i = lax.broadcasted_iota(jnp.int32, (tq, tq), 0)
                c_i = lax.broadcasted_iota(jnp.int32, (tq, tq), 1)
                s = jnp.where(c_i <= r_i, s, NEG)
            m_new = jnp.maximum(m, jnp.max(s, axis=-1, keepdims=True))
            alpha = jnp.exp(m - m_new)
            p = jnp.exp(s - m_new)
            l = alpha * l + jnp.sum(p, axis=-1, keepdims=True)
            acc = alpha * acc + _dot(p.astype(BF16), v_ref[pl.ds(off, tq), hs])
            out.append((m_new, l, acc))
        return tuple(out)

    init = tuple((jnp.full((tq, 1), -jnp.inf, F32), jnp.zeros((tq, 1), F32),
                  jnp.zeros((tq, HEAD_DIM), F32)) for _ in range(hpb))
    carry = lax.fori_loop(0, i, lambda j, c: step(j, c, False), init)
    carry = step(i, carry, True)
    for hh in range(hpb):
        _, l, acc = carry[hh]
        o_ref[:, hh * HEAD_DIM:(hh + 1) * HEAD_DIM] = (acc / l).astype(o_ref.dtype)


def _fox(qk, v, c, b_sz, t_len, n_heads):
    tq = _pick(t_len, (384, 256, 128))
    nq = t_len // tq
    hpb = _pick(n_heads, (8, 4, 2, 1))
    nhb = n_heads // hpb
    hbw = hpb * HEAD_DIM
    return pl.pallas_call(
        functools.partial(_fox_kernel, tq=tq, hpb=hpb),
        grid=(b_sz, nhb, nq),
        in_specs=[pl.BlockSpec((None, tq, hbw), lambda b, h, i: (b, i, h)),
                  pl.BlockSpec((None, t_len, hbw), lambda b, h, i: (b, 0, nhb + h)),
                  pl.BlockSpec((None, t_len, hbw), lambda b, h, i: (b, 0, h)),
                  pl.BlockSpec((None, hpb, 1, t_len), lambda b, h, i: (b, h, 0, 0))],
        out_specs=pl.BlockSpec((None, tq, hbw), lambda b, h, i: (b, i, h)),
        out_shape=jax.ShapeDtypeStruct((b_sz, t_len, n_heads * HEAD_DIM), BF16),
        compiler_params=_params("parallel", "parallel", "parallel"),
        name="fox_attention",
    )(qk, qk, v, c)


def _rel_bucket_int(rel):
    nb = REL_BUCKETS // 2
    max_exact = nb // 2
    ret = nb if rel > 0 else 0
    n = abs(rel)
    if n < max_exact:
        return ret + n
    span = nb - max_exact
    k = 0
    while (n ** span) * (max_exact ** (k + 1)) >= (REL_MAX_DIST ** (k + 1)) * (max_exact ** span):
        k += 1
        if max_exact + k >= nb - 1:
            break
    return ret + min(max_exact + k, nb - 1)


def _bucket_strips():
    far = _rel_bucket_int(-(LANE + 1))
    assert far == _rel_bucket_int(-(10 ** 6))
    q = np.arange(LANE)[:, None]
    k = np.arange(LANE)[None, :]
    lut = {r: _rel_bucket_int(r) for r in range(-3 * LANE, 3 * LANE)}
    f = np.vectorize(lambda r: lut[int(r)])
    own = f(k - q)
    prev = f(k - q - LANE)
    meta1 = f(k - (q + N_META))
    farblk = np.full((LANE, LANE), far)
    strips = np.stack([np.concatenate([own, farblk], 1),
                       np.concatenate([meta1, own], 1),
                       np.concatenate([prev, own], 1)])
    return strips.reshape(3, 1, LANE * 2 * LANE).astype(np.int32), far


def _bias_kernel(b_ref, r_ref, o_ref, *, far):
    bk = b_ref[...]
    w = bk.shape[1]
    oh = jnp.where(lax.broadcasted_iota(jnp.int32, (LANE, w), 0) == bk, 1.0, 0.0).astype(BF16)
    rb = r_ref[...]
    rb = rb - rb[:, far:far + 1]
    hi, mid, lo = _split3(rb)
    o_ref[...] = _dot(hi, oh) + _dot(mid, oh) + _dot(lo, oh)


def _bias_strips(rel_bias):
    n_heads = rel_bias.shape[1]
    hp = max(16, n_heads)
    strips, far = _bucket_strips()
    w_all = strips.shape[2]
    wc = 4096
    rbt = jnp.zeros((hp, LANE), F32).at[:n_heads, :REL_BUCKETS].set(rel_bias.T.astype(F32))
    out = pl.pallas_call(
        functools.partial(_bias_kernel, far=far),
        grid=(3, w_all // wc),
        in_specs=[pl.BlockSpec((None, 1, wc), lambda s, c: (s, 0, c)),
                  pl.BlockSpec((hp, LANE), lambda s, c: (0, 0))],
        out_specs=pl.BlockSpec((None, hp, wc), lambda s, c: (s, 0, c)),
        out_shape=jax.ShapeDtypeStruct((3, hp, w_all), F32),
        compiler_params=_params("parallel", "parallel"),
        name="rel_bias_strips",
    )(jnp.asarray(strips), rbt)
    return out[:, :n_heads].reshape(3, n_heads, LANE, 2 * LANE)


def _dsa_kernel(qi_ref, small_ref, kit_ref, qb_ref, kb_ref, vb_ref, bias_ref, o_ref,
                key_sc, *, kc, topk, n_heads):
    j = pl.program_id(1)
    bpc = kc // LANE
    nch = j // bpc + 1
    rep = n_heads // B_KV_HEADS

    lane = lax.broadcasted_iota(jnp.int32, (LANE, LANE), 1)
    lo_m = jnp.where(lane < IDX_DIM, 1.0, 0.0).astype(BF16)
    hi_m = jnp.where(lane >= IDX_DIM, 1.0, 0.0).astype(BF16)
    parts = []
    for p in range(IDX_HEADS // 2):
        blk = qi_ref[:, p * LANE:(p + 1) * LANE]
        parts.append(blk * lo_m)
        parts.append(blk * hi_m)
    lhs = jnp.concatenate(parts, axis=0)
    w_off = IDX_DIM + IDX_HEADS
    w = small_ref[:, w_off:w_off + IDX_HEADS] * ((IDX_HEADS ** -0.5) * (IDX_DIM ** -0.5))
    qrow = j * LANE + lax.broadcasted_iota(jnp.int32, (LANE, kc), 0)

    def idx_body(c, _):
        off = pl.multiple_of(c * kc, kc)
        d = _dot(lhs, kit_ref[:, pl.ds(off, kc)])
        acc = jnp.zeros((LANE, kc), F32)
        for h in range(IDX_HEADS):
            acc = acc + w[:, h:h + 1] * jnp.maximum(d[h * LANE:(h + 1) * LANE], 0.0)
        krow = off + lax.broadcasted_iota(jnp.int32, (LANE, kc), 1)
        adm = (krow < N_META) | ((krow >= LANE) & ((krow // CHUNK) <= (qrow // CHUNK)))
        isc = jnp.where(adm, acc, -jnp.inf)
        bits = pltpu.bitcast(isc, jnp.int32)
        key_sc[:, pl.ds(off, kc)] = bits ^ ((bits >> 31) & 0x7FFFFFFF)
        return 0

    lax.fori_loop(0, nch, idx_body, 0)

    def search(b, t_u):
        bit = jnp.left_shift(jnp.int32(1), 31 - b)
        cand_u = t_u | bit
        cand_s = cand_u ^ INT_MIN

        def cnt_body(c, cnt):
            off = pl.multiple_of(c * kc, kc)
            ge = jnp.where(key_sc[:, pl.ds(off, kc)] >= cand_s, 1, 0)
            for t in range(bpc):
                cnt = cnt + ge[:, t * LANE:(t + 1) * LANE]
            return cnt

        cnt = lax.fori_loop(0, nch, cnt_body, jnp.zeros((LANE, LANE), jnp.int32))
        tot = jnp.sum(cnt, axis=-1, keepdims=True)
        return jnp.where(tot >= topk, cand_u, t_u)

    t_u = lax.fori_loop(0, 32, search, jnp.zeros((LANE, 1), jnp.int32))
    thr = jnp.maximum(t_u ^ INT_MIN, NEGINF_KEY + 1)

    near = pl.multiple_of(jnp.maximum(j - 1, 0) * LANE, LANE)
    nfar = jnp.maximum(j - 2 + bpc, 0) // bpc
    far_limit = (j - 1) * LANE

    qgs = [jnp.concatenate(
        [qb_ref[:, (g * rep + r) * HEAD_DIM:(g * rep + r + 1) * HEAD_DIM] for r in range(rep)],
        axis=0) for g in range(B_KV_HEADS)]

    def attend(off, width, with_bias, limit, carries):
        sel = key_sc[:, pl.ds(off, width)] >= thr
        if limit is not None:
            krow = off + lax.broadcasted_iota(jnp.int32, (LANE, width), 1)
            sel = sel & (krow < limit)
        out = []
        for g in range(B_KV_HEADS):
            m, l, acc = carries[g]
            gsl = slice(g * HEAD_DIM, (g + 1) * HEAD_DIM)
            s = _dot_nt(qgs[g], kb_ref[pl.ds(off, width), gsl]).reshape(rep, LANE, width)
            if with_bias:
                s = s + bias_ref[g * rep:(g + 1) * rep]
            s = jnp.where(sel[None], s, NEG)
            m_new = jnp.maximum(m, jnp.max(s, axis=-1, keepdims=True))
            alpha = jnp.exp(m - m_new)
            p = jnp.exp(s - m_new)
            l = alpha * l + jnp.sum(p, axis=-1, keepdims=True)
            pv = _dot(p.reshape(rep * LANE, width).astype(BF16), vb_ref[pl.ds(off, width), gsl])
            acc = alpha * acc + pv.reshape(rep, LANE, HEAD_DIM)
            out.append((m_new, l, acc))
        return tuple(out)

    init = tuple((jnp.full((rep, LANE, 1), -jnp.inf, F32), jnp.zeros((rep, LANE, 1), F32),
                  jnp.zeros((rep, LANE, HEAD_DIM), F32)) for _ in range(B_KV_HEADS))
    carries = attend(near, 2 * LANE, True, None, init)

    def far_body(c, carries):
        return attend(pl.multiple_of(c * kc, kc), kc, False, far_limit, carries)

    carries = lax.fori_loop(0, nfar, far_body, carries)
    for g in range(B_KV_HEADS):
        _, l, acc = carries[g]
        out = acc / l
        for r in range(rep):
            o_ref[:, (g * rep + r) * HEAD_DIM:(g * rep + r + 1) * HEAD_DIM] = out[r].astype(o_ref.dtype)


def _dsa(qi, small, kit, qk, kb, vb, bias, b_sz, t_len, n_heads, topk):
    nb = t_len // LANE
    kc = _pick(t_len, (384, 256))
    assert kc >= topk and t_len >= 2 * LANE
    hw = n_heads * HEAD_DIM
    kw = B_KV_HEADS * HEAD_DIM
    rowmap = lambda b, j: (b * nb + j, 0)
    return pl.pallas_call(
        functools.partial(_dsa_kernel, kc=kc, topk=topk, n_heads=n_heads),
        grid=(b_sz, nb),
        in_specs=[pl.BlockSpec((LANE, IDX_HEADS * IDX_DIM), rowmap),
                  pl.BlockSpec((LANE, LANE), rowmap),
                  pl.BlockSpec((None, 2 * IDX_DIM, t_len), lambda b, j: (b, 0, 0)),
                  pl.BlockSpec((LANE, hw), lambda b, j: (b * nb + j, 2)),
                  pl.BlockSpec((None, t_len, kw), lambda b, j: (b, 0, 0)),
                  pl.BlockSpec((None, t_len, kw), lambda b, j: (b, 0, 0)),
                  pl.BlockSpec((None, n_heads, LANE, 2 * LANE),
                               lambda b, j: (jnp.minimum(j, 2), 0, 0, 0))],
        out_specs=pl.BlockSpec((LANE, hw), rowmap),
        out_shape=jax.ShapeDtypeStruct((qi.shape[0], hw), BF16),
        scratch_shapes=[pltpu.VMEM((LANE, t_len), jnp.int32)],
        compiler_params=_params("parallel", "parallel"),
        name="dsa_attention",
    )(qi, small, kit, qk, kb, vb, bias)


def _norm_router_kernel(x_ref, g_ref, r_ref, o_ref, gate_ref, *, n_exp):
    x = x_ref[...]
    ms = jnp.mean(x * x, axis=-1, keepdims=True)
    y = x * lax.rsqrt(ms + EPS) * g_ref[...]
    o_ref[...] = y.astype(o_ref.dtype)
    logits = [jnp.sum(y * r_ref[e:e + 1, :], axis=-1, keepdims=True) for e in range(n_exp)]

    def argmax(vals):
        best, bi = vals[0], jnp.zeros_like(vals[0], dtype=jnp.int32)
        for e in range(1, n_exp):
            upd = vals[e] > best
            best = jnp.where(upd, vals[e], best)
            bi = jnp.where(upd, e, bi)
        return best, bi

    m1, i1 = argmax(logits)
    m2, i2 = argmax([jnp.where(i1 == e, -jnp.inf, logits[e]) for e in range(n_exp)])
    t = jnp.exp(m2 - m1)
    w1 = 1.0 / (1.0 + t)
    w2 = t / (1.0 + t)
    lane = lax.broadcasted_iota(jnp.int32, gate_ref.shape, 1)
    gate_ref[...] = jnp.where(lane == i1, w1, jnp.where(lane == i2, w2, 0.0))


def _norm_router(x, g, router):
    m, d = x.shape
    n_exp = router.shape[1]
    tm = _pick(m, (256, 128))
    return pl.pallas_call(
        functools.partial(_norm_router_kernel, n_exp=n_exp),
        grid=(m // tm,),
        in_specs=[pl.BlockSpec((tm, d), lambda i: (i, 0)),
                  pl.BlockSpec((1, d), lambda i: (0, 0)),
                  pl.BlockSpec((n_exp, d), lambda i: (0, 0))],
        out_specs=[pl.BlockSpec((tm, d), lambda i: (i, 0)),
                   pl.BlockSpec((tm, LANE), lambda i: (i, 0))],
        out_shape=[jax.ShapeDtypeStruct((m, d), BF16),
                   jax.ShapeDtypeStruct((m, LANE), F32)],
        compiler_params=_params("parallel"),
        name="norm_router",
    )(x, g.reshape(1, d), router.T.astype(F32))


def _dense_ffn(h2, h_res, w1, w3, w2):
    m, d = h2.shape
    f = w1.shape[1]
    fp = ((f + 1023) // 1024) * 1024 if f > 1024 else f
    pad = ((0, 0), (0, fp - f))
    w1p = jnp.pad(w1, pad).astype(BF16)
    w3p = jnp.pad(w3, pad).astype(BF16)
    w2p = jnp.pad(w2, ((0, fp - f), (0, 0))).astype(BF16)
    tm, tn, tk = _mm_tiles(m, fp, d)
    wspec = pl.BlockSpec((tk, tn), lambda i, j, k: (k, j))
    a = _mm([(h2, pl.BlockSpec((tm, tk), lambda i, j, k: (i, k)))],
            [(w1p, wspec), (w3p, wspec)], [0, 0], [], _ep_swiglu,
            jax.ShapeDtypeStruct((m, fp), BF16),
            pl.BlockSpec((tm, tn), lambda i, j, k: (i, j)),
            (m // tm, fp // tn, d // tk), tm, tn, "ffn_up")
    return _matmul(a, w2p, _ep_residual,
                   [(h_res, lambda tm, tn: (tm, tn), lambda i, j, k: (i, j))], F32, "ffn_down")


MOE_CAP = 384


def _moe_up_kernel(cnt_ref, h_ref, rr_ref, w1_ref, w3_ref, a_ref, xs_sc, *, cap, n_exp, cw):
    i, e, c, f = (pl.program_id(a) for a in range(4))
    active = (c == 0) | (c * cap < cnt_ref[i * n_exp + e])

    @pl.when(active & (f == 0))
    def _():
        rr = rr_ref[...]
        sel = jnp.where(rr - c * cap == lax.broadcasted_iota(jnp.int32, (cap, rr.shape[1]), 0),
                        1.0, 0.0).astype(BF16)
        for cc in range(h_ref.shape[1] // cw):
            xs_sc[:, cc * cw:(cc + 1) * cw] = _dot(sel, h_ref[:, cc * cw:(cc + 1) * cw]).astype(BF16)

    @pl.when(active)
    def _():
        xs = xs_sc[...]
        a = _dot(xs, w1_ref[...])
        b = _dot(xs, w3_ref[...])
        a_ref[...] = (a * _sigmoid(a) * b).astype(a_ref.dtype)

    @pl.when(jnp.logical_not(active))
    def _():
        a_ref[...] = jnp.zeros_like(a_ref)


def _moe_down_kernel(cnt_ref, a_ref, w2_ref, rk_ref, g_ref, res_ref, o_ref, acc, *, cap, n_exp, n_chunk):
    i, e, c = pl.program_id(0), pl.program_id(2), pl.program_id(3)
    active = (c == 0) | (c * cap < cnt_ref[i * n_exp + e])

    @pl.when((e == 0) & (c == 0))
    def _():
        acc[...] = res_ref[...]

    @pl.when(active)
    def _():
        y = _dot(a_ref[...], w2_ref[...]).astype(BF16)
        lane = lax.broadcasted_iota(jnp.int32, rk_ref.shape, 1)
        rank = jnp.sum(jnp.where(lane == e, rk_ref[...], 0.0), axis=-1, keepdims=True)
        gate = jnp.sum(jnp.where(lane == e, g_ref[...], 0.0), axis=-1, keepdims=True)
        tm = rk_ref.shape[0]
        scat = jnp.where(rank.astype(jnp.int32) - c * cap
                         == lax.broadcasted_iota(jnp.int32, (tm, cap), 1), 1.0, 0.0).astype(BF16)
        acc[...] += gate * _dot(scat, y)

    @pl.when((e == n_exp - 1) & (c == n_chunk - 1))
    def _():
        o_ref[...] = acc[...]


def _moe_ffn(h2, gates, h_res, w1, w3, w2):
    m, d = h2.shape
    n_exp, _, f = w1.shape
    tm = _pick(m, (1408, 768, 512, 256, 128))
    tn = _pick(f, (512, 256, 128))
    cap = min(MOE_CAP, tm)
    n_chunk = -(-tm // cap)
    n_i, n_f, n_n = m // tm, f // tn, d // tn

    sel = (gates[:, :n_exp] > 0).reshape(n_i, tm, n_exp)
    seli = sel.astype(jnp.int32)
    rank = jnp.where(sel, jnp.cumsum(seli, axis=1) - seli, -1).reshape(m, n_exp)
    cnt = jnp.sum(seli, axis=1).reshape(n_i * n_exp)
    rank_lanes = jnp.full((m, LANE), -1.0, F32).at[:, :n_exp].set(rank.astype(F32))
    rank_rows = rank.T.reshape(n_exp, 1, m)

    def act(i, e, c, cnt_ref):
        return (c == 0) | (c * cap < cnt_ref[i * n_exp + e])

    def wmap(i, e, c, j, cnt_ref):
        return (e, 0, jnp.where(act(i, e, c, cnt_ref), j, n_f - 1))

    a = pl.pallas_call(
        functools.partial(_moe_up_kernel, cap=cap, n_exp=n_exp, cw=_pick(d, (1024, 512, 256, 128))),
        grid_spec=pltpu.PrefetchScalarGridSpec(
            num_scalar_prefetch=1,
            grid=(n_i, n_exp, n_chunk, n_f),
            in_specs=[pl.BlockSpec((tm, d), lambda i, e, c, j, s: (i, 0)),
                      pl.BlockSpec((None, 1, tm), lambda i, e, c, j, s: (e, 0, i)),
                      pl.BlockSpec((None, d, tn), wmap),
                      pl.BlockSpec((None, d, tn), wmap)],
            out_specs=pl.BlockSpec((cap, tn), lambda i, e, c, j, s: ((i * n_exp + e) * n_chunk + c, j)),
            scratch_shapes=[pltpu.VMEM((cap, d), BF16)]),
        out_shape=jax.ShapeDtypeStruct((n_i * n_exp * n_chunk * cap, f), BF16),
        compiler_params=_params("parallel", "arbitrary", "arbitrary", "arbitrary"),
        name="moe_up",
    )(cnt, h2, rank_rows, w1.astype(BF16), w3.astype(BF16))

    def amap(i, n, e, c, cnt_ref):
        return ((i * n_exp + e) * n_chunk + jnp.where(act(i, e, c, cnt_ref), c, 0), 0)

    return pl.pallas_call(
        functools.partial(_moe_down_kernel, cap=cap, n_exp=n_exp, n_chunk=n_chunk),
        grid_spec=pltpu.PrefetchScalarGridSpec(
            num_scalar_prefetch=1,
            grid=(n_i, n_n, n_exp, n_chunk),
            in_specs=[pl.BlockSpec((cap, f), amap),
                      pl.BlockSpec((None, f, tn), lambda i, n, e, c, s: (e, 0, n)),
                      pl.BlockSpec((tm, LANE), lambda i, n, e, c, s: (i, 0)),
                      pl.BlockSpec((tm, LANE), lambda i, n, e, c, s: (i, 0)),
                      pl.BlockSpec((tm, tn), lambda i, n, e, c, s: (i, n))],
            out_specs=pl.BlockSpec((tm, tn), lambda i, n, e, c, s: (i, n)),
            scratch_shapes=[pltpu.VMEM((tm, tn), F32)]),
        out_shape=jax.ShapeDtypeStruct((m, d), F32),
        compiler_params=_params("parallel", "parallel", "arbitrary", "arbitrary"),
        name="moe_down",
    )(cnt, a, w2.astype(BF16), rank_lanes, gates, h_res)


def kernel(x, meta_tokens, rel_bias, norm_mix, w_in, b_forget, q_norm_a, k_norm_a, kv_norm_b, w_ukv_b, q_norm_b, k_norm_b, w_proj_a, w_proj_b, w_gate, b_gate, w_out, norm_ffn, dense_w1, dense_w3, dense_w2, router, moe_w1, moe_w3, moe_w2):
    b_sz, s_len, d = x.shape
    depth = norm_mix.shape[0]
    n_heads = d // (2 * HEAD_DIM)
    hw = n_heads * HEAD_DIM
    latent = kv_norm_b.shape[1]
    iqw = IDX_HEADS * IDX_DIM
    assert s_len % LANE == 0 and N_META <= LANE and n_heads % B_KV_HEADS == 0
    t_len = LANE + s_len
    m = b_sz * t_len
    topk = min(TOPK_MAX, s_len // 4)
    scale = HEAD_DIM ** -0.5

    sizes = (hw, hw, hw, n_heads, hw, latent, iqw, IDX_DIM, IDX_HEADS)
    offs = np.concatenate([[0], np.cumsum(sizes)])
    o_qa, o_ka, o_va, o_fa, o_qb, o_cb, o_qi, o_ki, o_wi = (int(v) for v in offs[:-1])

    meta = jnp.broadcast_to(meta_tokens[None].astype(F32), (b_sz, N_META, d))
    h_res = jnp.concatenate(
        [meta, jnp.zeros((b_sz, LANE - N_META, d), F32), x.astype(F32)], axis=1).reshape(m, d)

    bias = _bias_strips(rel_bias)

    for l in range(depth):
        w = w_in[l]
        w_qkq = jnp.concatenate([w[:, o_qa:o_qa + hw], w[:, o_ka:o_ka + hw], w[:, o_qb:o_qb + hw]],
                                axis=1).astype(BF16)
        w_va = w[:, o_va:o_va + hw].astype(BF16)
        w_qi = w[:, o_qi:o_qi + iqw].astype(BF16)
        w_cb = w[:, o_cb:o_cb + latent].astype(BF16)
        n_small = IDX_DIM + n_heads + IDX_HEADS
        assert n_small <= LANE and IDX_DIM + n_heads <= IDX_DIM + IDX_HEADS
        w_small = jnp.concatenate(
            [w[:, o_ki:o_ki + IDX_DIM], w[:, o_fa:o_fa + n_heads],
             jnp.zeros((d, IDX_HEADS - n_heads), F32), w[:, o_wi:o_wi + IDX_HEADS],
             jnp.zeros((d, LANE - IDX_DIM - 2 * IDX_HEADS), F32)], axis=1).astype(BF16)
        gains = jnp.concatenate([jnp.tile(q_norm_a[l] * scale, n_heads), jnp.tile(k_norm_a[l], n_heads),
                                 jnp.tile(q_norm_b[l] * scale, n_heads)]).reshape(1, 3 * hw).astype(F32)
        brow = jnp.zeros((1, LANE), F32).at[0, IDX_DIM:IDX_DIM + n_heads].set(b_forget[l].astype(F32))

        h = _rmsnorm(h_res, norm_mix[l])
        qkq = _matmul(h, w_qkq, _ep_headnorm,
                      [(gains, lambda tm, tn: (1, tn), lambda i, j, k: (0, j))], BF16, "proj_qkq")
        va = _matmul(h, w_va, _ep_plain, [], BF16, "proj_va")
        qi = _matmul(h, w_qi, _ep_plain, [], BF16, "proj_qi")
        cb = _matmul(h, w_cb, _ep_plain, [], BF16, "proj_cb")
        small = _matmul(h, w_small, _ep_plain, [], F32, "proj_small")
        gate = _matmul(h, w_gate[l].astype(BF16), _ep_gate,
                       [(b_gate[l].reshape(1, 2 * d).astype(F32), lambda tm, tn: (1, tn),
                         lambda i, j, k: (0, j))], BF16, "proj_gate")

        c, kit, kb, vb = _post(small, cb, brow, kv_norm_b[l].reshape(1, latent).astype(F32),
                               w_ukv_b[l].astype(BF16), k_norm_b[l].reshape(1, HEAD_DIM).astype(F32),
                               b_sz, t_len, n_heads)

        oa = _fox(qkq.reshape(b_sz, t_len, 3 * hw), va.reshape(b_sz, t_len, hw),
                  c.reshape(b_sz, n_heads, 1, t_len), b_sz, t_len, n_heads).reshape(m, hw)
        kw = B_KV_HEADS * HEAD_DIM
        ob = _dsa(qi, small, kit, qkq, kb.reshape(b_sz, t_len, kw), vb.reshape(b_sz, t_len, kw),
                  bias, b_sz, t_len, n_heads, topk)

        tm, tn, tk = _mm_tiles(m, d, hw)
        nj = d // tn
        xspec = pl.BlockSpec((tm, tk), lambda i, j, k: (i, k))
        wspec = pl.BlockSpec((tk, tn), lambda i, j, k: (k, j))
        merged = _mm([(oa, xspec), (ob, xspec)],
                     [(w_proj_a[l].astype(BF16), wspec), (w_proj_b[l].astype(BF16), wspec)], [0, 1],
                     [(gate, pl.BlockSpec((tm, tn), lambda i, j, k: (i, j))),
                      (gate, pl.BlockSpec((tm, tn), lambda i, j, k: (i, j + nj)))],
                     _ep_merge, jax.ShapeDtypeStruct((m, d), BF16),
                     pl.BlockSpec((tm, tn), lambda i, j, k: (i, j)),
                     (m // tm, nj, hw // tk), tm, tn, "merge_proj")
        h_res = _matmul(merged, w_out[l].astype(BF16), _ep_residual,
                        [(h_res, lambda tm, tn: (tm, tn), lambda i, j, k: (i, j))], F32, "out_proj")

        jj = l // 2
        if l % 2 == 0:
            h2 = _rmsnorm(h_res, norm_ffn[l])
            h_res = _dense_ffn(h2, h_res, dense_w1[jj], dense_w3[jj], dense_w2[jj])
        else:
            h2, gates = _norm_router(h_res, norm_ffn[l], router[jj])
            h_res = _moe_ffn(h2, gates, h_res, moe_w1[jj], moe_w3[jj], moe_w2[jj])

    return h_res.reshape(b_sz, t_len, d)[:, LANE:, :]
```
